```python
import math
import jax, jax.numpy as jnp
from jax import lax
import numpy as np

D_MODEL = 2048
BATCH = 2
SEQ = 4096
DEPTH = 4
DEC_BATCH = 32
DEC_SEQ = 1
PAST_LEN = 16384
PAGE_SIZE = 128

N_MIXERS = 2
N_ATTN_LAYERS = (DEPTH + 1) // 2
N_CONV_LAYERS = DEPTH // 2
N_HEADS = 64
N_KV_HEADS = 8
HEAD_DIM = 64
GROUP = N_HEADS // N_KV_HEADS
Q_DIM = N_HEADS * HEAD_DIM
KV_DIM = N_KV_HEADS * HEAD_DIM
WINDOW = 128
NUM_BUCKETS = 32
MAX_DISTANCE = 128
CONV_WIDTH = 31
PEER_HEADS = 8
N_KEYS = 128
N_EXPERTS = N_KEYS * N_KEYS
PEER_TOPK = 16
D_KEY = 256
D_KEY_HALF = D_KEY // 2
PEER_BLOCK = 128
EPS = 1e-6

kernel_name = 'hybrid_swa_conformer_peer_adaln_step'


def rms_norm(x, g):
    xf = x.astype(jnp.float32)
    y = xf * lax.rsqrt(jnp.mean(xf * xf, axis=-1, keepdims=True) + EPS)
    return (y * g.astype(jnp.float32)).astype(x.dtype)


def layer_norm(x, g, b):
    xf = x.astype(jnp.float32)
    mu = jnp.mean(xf, axis=-1, keepdims=True)
    var = jnp.mean(jnp.square(xf - mu), axis=-1, keepdims=True)
    y = (xf - mu) * lax.rsqrt(var + EPS)
    return (y * g.astype(jnp.float32) + b.astype(jnp.float32)).astype(x.dtype)


def t5_bucket(rel):
    n = jnp.maximum(rel, 0)
    max_exact = NUM_BUCKETS // 2
    nf = jnp.maximum(n, 1).astype(jnp.float32)
    large = max_exact + (jnp.log(nf / max_exact) / math.log(MAX_DISTANCE / max_exact)
                         * (NUM_BUCKETS - max_exact)).astype(jnp.int32)
    large = jnp.minimum(large, NUM_BUCKETS - 1)
    return jnp.where(n < max_exact, n, large)


def banded_attention(q, k, v, base, q_off, k_off, rel_bias, sinks):
    B, NB, TQ = q.shape[:3]
    TK = k.shape[2]
    qg = q.reshape(B, NB, TQ, N_KV_HEADS, GROUP, HEAD_DIM)
    s = jnp.einsum('bnqkgd,bnskd->bnkgqs', qg, k).astype(jnp.float32) * (HEAD_DIM ** -0.5)
    rel = q_off[:, None] - k_off[None, :]
    bias = rel_bias.astype(jnp.float32)[t5_bucket(rel)]
    bias = bias.reshape(TQ, TK, N_KV_HEADS, GROUP).transpose(2, 3, 0, 1)
    k_abs = base[:, None] + k_off[None, :]
    valid = ((rel >= 0) & (rel < WINDOW))[None] & (k_abs >= 0)[:, None, :]
    s = jnp.where(valid[None, :, None, None], s + bias, -jnp.inf)
    sink = sinks.astype(jnp.float32).reshape(1, 1, N_KV_HEADS, GROUP, 1, 1)
    m = jnp.maximum(jnp.max(s, axis=-1, keepdims=True), sink)
    p = jnp.exp(s - m)
    w = p / (jnp.sum(p, axis=-1, keepdims=True) + jnp.exp(sink - m))
    o = jnp.einsum('bnkgqs,bnskd->bnqkgd', w.astype(v.dtype), v)
    return o.reshape(B, NB, TQ, Q_DIM)


def window_attn_mixer(h, w_qkv, b_qkv, w_o, sinks, rel_bias, cache_k, cache_v):
    B, T, _ = h.shape
    qkv = h @ w_qkv + b_qkv
    q = qkv[..., :Q_DIM].reshape(B, T, N_HEADS, HEAD_DIM)
    k = qkv[..., Q_DIM:Q_DIM + KV_DIM].reshape(B, T, N_KV_HEADS, HEAD_DIM)
    v = qkv[..., Q_DIM + KV_DIM:].reshape(B, T, N_KV_HEADS, HEAD_DIM)
    if cache_k is None:
        nb = T // WINDOW
        q_blk = q.reshape(B, nb, WINDOW, N_HEADS, HEAD_DIM)

        def band(z):
            zp = jnp.concatenate([jnp.zeros_like(z[:, :WINDOW]), z], axis=1)
            zp = zp.reshape(B, nb + 1, WINDOW, N_KV_HEADS, HEAD_DIM)
            return jnp.concatenate([zp[:, :-1], zp[:, 1:]], axis=2)

        k_blk, v_blk = band(k), band(v)
        base = jnp.arange(nb, dtype=jnp.int32) * WINDOW
        q_off = jnp.arange(WINDOW, dtype=jnp.int32)
        k_off = jnp.arange(2 * WINDOW, dtype=jnp.int32) - WINDOW
        k_all, v_all = k, v
    else:
        k_all = jnp.concatenate([cache_k, k], axis=1)
        v_all = jnp.concatenate([cache_v, v], axis=1)
        q_blk, k_blk, v_blk = q[:, None], k_all[:, None], v_all[:, None]
        base = jnp.full((1,), PAST_LEN, dtype=jnp.int32)
        q_off = jnp.arange(T, dtype=jnp.int32)
        k_off = jnp.arange(WINDOW + T, dtype=jnp.int32) - WINDOW
    o = banded_attention(q_blk, k_blk, v_blk, base, q_off, k_off, rel_bias, sinks)
    out = o.reshape(B, T, Q_DIM) @ w_o
    return out, k_all[:, -WINDOW:], v_all[:, -WINDOW:]


def conv_module(h, w_pw1, b_pw1, w_dw, b_dw, ln_g, ln_b, w_pw2, b_pw2, state):
    B, T, _ = h.shape
    a, g = jnp.split(h @ w_pw1 + b_pw1, 2, axis=-1)
    u = a * jax.nn.sigmoid(g)
    prefix = jnp.zeros((B, CONV_WIDTH - 1, D_MODEL), u.dtype) if state is None else state.astype(u.dtype)
    full = jnp.concatenate([prefix, u], axis=1)
    y = lax.conv_general_dilated(full, w_dw[:, None, :].astype(full.dtype), window_strides=(1,),
                                 padding='VALID', dimension_numbers=('NWC', 'WIO', 'NWC'),
                                 feature_group_count=D_MODEL) + b_dw
    y = layer_norm(y, ln_g, ln_b)
    y = y * jax.nn.sigmoid(y)
    return y @ w_pw2 + b_pw2, full[:, -(CONV_WIDTH - 1):]


def peer_block(x, w_pq, sub_keys, u_tab, v_tab):
    T = x.shape[0]
    q = (x @ w_pq).reshape(T, PEER_HEADS, 2, D_KEY_HALF)
    s = jnp.einsum('thpc,hpnc->thpn', q, sub_keys).astype(jnp.float32)
    s_top, i_top = lax.top_k(s, PEER_TOPK)
    cand_s = (s_top[:, :, 0, :, None] + s_top[:, :, 1, None, :]).reshape(T, PEER_HEADS, PEER_TOPK * PEER_TOPK)
    cand_e = (i_top[:, :, 0, :, None] * N_KEYS + i_top[:, :, 1, None, :]).reshape(T, PEER_HEADS, PEER_TOPK * PEER_TOPK)
    best_s, best_j = lax.top_k(cand_s, PEER_TOPK)
    expert = jnp.take_along_axis(cand_e, best_j, axis=-1)
    gate = jax.nn.softmax(best_s, axis=-1)
    act = jax.nn.gelu(jnp.einsum('thkd,td->thk', u_tab[expert], x).astype(jnp.float32), approximate=False)
    w = (gate * act).astype(x.dtype)
    return jnp.einsum('thk,thkd->td', w, v_tab[expert])


def peer(h, w_pq, sub_keys, u_tab, v_tab):
    B, T, D = h.shape
    n = B * T
    blk = min(PEER_BLOCK, n)
    pad = (-n) % blk
    xt = jnp.pad(h.reshape(n, D), ((0, pad), (0, 0))).reshape(-1, blk, D)
    yt = lax.map(lambda xb: peer_block(xb, w_pq, sub_keys, u_tab, v_tab), xt)
    return yt.reshape(-1, D)[:n].reshape(B, T, D)


def trunk(x, c, cache_k, cache_v, state_conv, rel_bias, w_ada, b_ada, g_mix, g_ffn, g_final,
          w_qkv, b_qkv, w_o, sinks, w_pw1, b_pw1, w_dw, b_dw, ln_g, ln_b, w_pw2, b_pw2,
          w_pq, sub_keys, u_tab, v_tab):
    new_k, new_v, new_conv = [], [], []
    cond = jax.nn.silu(c)
    for i in range(DEPTH):
        mod = (cond @ w_ada[i] + b_ada[i])[:, None, :]
        sh_m, sc_m, gt_m, sh_f, sc_f, gt_f = jnp.split(mod, 6, axis=-1)
        h = rms_norm(x, g_mix[i]) * (1 + sc_m) + sh_m
        if i % N_MIXERS == 0:
            a = i // N_MIXERS
            out, k_rows, v_rows = window_attn_mixer(
                h, w_qkv[a], b_qkv[a], w_o[a], sinks[a], rel_bias,
                None if cache_k is None else cache_k[a], None if cache_v is None else cache_v[a])
            new_k.append(k_rows)
            new_v.append(v_rows)
        else:
            b = i // N_MIXERS
            out, conv_rows = conv_module(h, w_pw1[b], b_pw1[b], w_dw[b], b_dw[b], ln_g[b], ln_b[b],
                                         w_pw2[b], b_pw2[b], None if state_conv is None else state_conv[b])
            new_conv.append(conv_rows)
        x = x + gt_m * out
        h = rms_norm(x, g_ffn[i]) * (1 + sc_f) + sh_f
        x = x + gt_f * peer(h, w_pq[i], sub_keys[i], u_tab[i], v_tab[i])
    return rms_norm(x, g_final), jnp.stack(new_k), jnp.stack(new_v), jnp.stack(new_conv)


def setup_inputs(seed: int = 0) -> dict:
    key = jax.random.key(seed)
    ks = jax.random.split(key, 32)
    D = D_MODEL

    def nrm(k, shape, scale):
        return jax.random.normal(k, shape, jnp.float32) * scale

    return {
        'x_prompt': nrm(ks[0], (BATCH, SEQ, D), 1.0),
        'x_sample': nrm(ks[1], (DEC_BATCH, DEC_SEQ, D), 1.0),
        'cache_k': nrm(ks[2], (N_ATTN_LAYERS, DEC_BATCH, WINDOW, N_KV_HEADS, HEAD_DIM), 1.0),
        'cache_v': nrm(ks[3], (N_ATTN_LAYERS, DEC_BATCH, WINDOW, N_KV_HEADS, HEAD_DIM), 1.0),
        'state_conv': nrm(ks[4], (N_CONV_LAYERS, DEC_BATCH, CONV_WIDTH - 1, D), 0.5),
        'c_prompt': nrm(ks[5], (BATCH, D), 1.0),
        'c_sample': nrm(ks[6], (DEC_BATCH, D), 1.0),
        'rel_bias': nrm(ks[7], (NUM_BUCKETS, N_HEADS), 0.5),
        'w_ada': nrm(ks[8], (DEPTH, D, 6 * D), 0.5 * D ** -0.5),
        'b_ada': nrm(ks[9], (DEPTH, 6 * D), 0.01),
        'g_mix': 1.0 + nrm(ks[10], (DEPTH, D), 0.05),
        'g_ffn': 1.0 + nrm(ks[11], (DEPTH, D), 0.05),
        'g_final': 1.0 + nrm(ks[12], (D,), 0.05),
        'w_qkv': nrm(ks[13], (N_ATTN_LAYERS, D, Q_DIM + 2 * KV_DIM), D ** -0.5),
        'b_qkv': nrm(ks[14], (N_ATTN_LAYERS, Q_DIM + 2 * KV_DIM), 0.01),
        'w_o': nrm(ks[15], (N_ATTN_LAYERS, Q_DIM, D), Q_DIM ** -0.5),
        'sinks': nrm(ks[16], (N_ATTN_LAYERS, N_HEADS), 0.5),
        'w_pw1': nrm(ks[17], (N_CONV_LAYERS, D, 2 * D), D ** -0.5),
        'b_pw1': nrm(ks[18], (N_CONV_LAYERS, 2 * D), 0.01),
        'w_dw': nrm(ks[19], (N_CONV_LAYERS, CONV_WIDTH, D), CONV_WIDTH ** -0.5),
        'b_dw': nrm(ks[20], (N_CONV_LAYERS, D), 0.01),
        'ln_g': 1.0 + nrm(ks[21], (N_CONV_LAYERS, D), 0.05),
        'ln_b': nrm(ks[22], (N_CONV_LAYERS, D), 0.01),
        'w_pw2': nrm(ks[23], (N_CONV_LAYERS, D, D), D ** -0.5),
        'b_pw2': nrm(ks[24], (N_CONV_LAYERS, D), 0.01),
        'w_pq': nrm(ks[25], (DEPTH, D, PEER_HEADS * D_KEY), D ** -0.5),
        'sub_keys': nrm(ks[26], (DEPTH, PEER_HEADS, 2, N_KEYS, D_KEY_HALF), D_KEY_HALF ** -0.5),
        'u_tab': nrm(ks[27], (DEPTH, N_EXPERTS, D), D ** -0.5),
        'v_tab': nrm(ks[28], (DEPTH, N_EXPERTS, D), 0.5),
    }


def reference(x_prompt, x_sample, cache_k, cache_v, state_conv, c_prompt, c_sample, rel_bias,
              w_ada, b_ada, g_mix, g_ffn, g_final, w_qkv, b_qkv, w_o, sinks,
              w_pw1, b_pw1, w_dw, b_dw, ln_g, ln_b, w_pw2, b_pw2, w_pq, sub_keys, u_tab, v_tab):
    y_prompt, new_k_prompt, new_v_prompt, new_conv_prompt = trunk(
        x_prompt, c_prompt, None, None, None, rel_bias, w_ada, b_ada, g_mix, g_ffn, g_final,
        w_qkv, b_qkv, w_o, sinks, w_pw1, b_pw1, w_dw, b_dw, ln_g, ln_b, w_pw2, b_pw2,
        w_pq, sub_keys, u_tab, v_tab)
    y_sample, new_k_sample, new_v_sample, new_conv_sample = trunk(
        x_sample, c_sample, cache_k, cache_v, state_conv, rel_bias, w_ada, b_ada, g_mix, g_ffn, g_final,
        w_qkv, b_qkv, w_o, sinks, w_pw1, b_pw1, w_dw, b_dw, ln_g, ln_b, w_pw2, b_pw2,
        w_pq, sub_keys, u_tab, v_tab)
    return (y_prompt, y_sample, new_k_prompt, new_v_prompt, new_conv_prompt,
            new_k_sample, new_v_sample, new_conv_sample)
```

```python
import functools
import math

import jax
import jax.numpy as jnp
import numpy as np
from jax import lax
from jax.experimental import pallas as pl
from jax.experimental.pallas import tpu as pltpu

D_MODEL = 2048
DEPTH = 4
N_HEADS = 64
N_KV_HEADS = 8
HEAD_DIM = 64
GROUP = N_HEADS // N_KV_HEADS
Q_DIM = N_HEADS * HEAD_DIM
KV_DIM = N_KV_HEADS * HEAD_DIM
WINDOW = 128
NUM_BUCKETS = 32
MAX_DISTANCE = 128
CONV_WIDTH = 31
PEER_HEADS = 8
N_KEYS = 128
N_EXPERTS = N_KEYS * N_KEYS
PEER_TOPK = 16
D_KEY_HALF = 128
EPS = 1e-6
NEG = -1e30

VMEM_LIMIT_V7X = 56 * 1024 * 1024
SAMPLE_PAD = 128

BF16 = jnp.bfloat16
F32 = jnp.float32


def _cparams(sem):
    return pltpu.CompilerParams(dimension_semantics=sem, vmem_limit_bytes=VMEM_LIMIT_V7X)


def _dot(a, b):
    return jnp.dot(a, b, preferred_element_type=F32)


def _dot_nt(a, b):
    return lax.dot_general(a, b, (((1,), (1,)), ((), ())), preferred_element_type=F32)


def _dot_tn(a, b):
    return lax.dot_general(a, b, (((0,), (0,)), ((), ())), preferred_element_type=F32)


def _ada_kernel(c_ref, w_ref, b_ref, o_ref):
    c = c_ref[...]
    cond = (c * jax.nn.sigmoid(c)).astype(BF16)
    o_ref[...] = _dot(cond, w_ref[...].astype(BF16)) + b_ref[...]


def ada_mod(c_all, w_ada, b_ada):
    rows = c_all.shape[0]
    n = w_ada.shape[2]
    tn = 1536
    return pl.pallas_call(
        _ada_kernel,
        grid=(DEPTH, n // tn),
        in_specs=[
            pl.BlockSpec((rows, D_MODEL), lambda l, j: (0, 0)),
            pl.BlockSpec((None, D_MODEL, tn), lambda l, j: (l, 0, j)),
            pl.BlockSpec((None, 1, tn), lambda l, j: (l, 0, j)),
        ],
        out_specs=pl.BlockSpec((None, rows, tn), lambda l, j: (l, 0, j)),
        out_shape=jax.ShapeDtypeStruct((DEPTH, rows, n), F32),
        compiler_params=_cparams(("arbitrary", "arbitrary")),
        name="ada_mod",
    )(c_all, w_ada, b_ada.reshape(DEPTH, 1, n))


def _norm_kernel(*refs, modulated, transposed):
    if modulated:
        x_ref, g_ref, sc_ref, sh_ref = refs[:4]
        outs = refs[4:]
    else:
        x_ref, g_ref = refs[:2]
        outs = refs[2:]
    x = x_ref[0]
    y = x * lax.rsqrt(jnp.mean(x * x, axis=-1, keepdims=True) + EPS) * g_ref[...]
    if modulated:
        y = y * (1.0 + sc_ref[0]) + sh_ref[0]
    outs[0][0] = y.astype(outs[0].dtype)
    if transposed:
        outs[1][...] = y.T.astype(outs[1].dtype)


def norm_mod(x, g, sc=None, sh=None, *, out_dtype=BF16, transposed=False):
    G, R, D = x.shape
    tt = min(R, 512)
    nt = R // tt
    modulated = sc is not None
    in_specs = [
        pl.BlockSpec((1, tt, D), lambda b, i: (b, i, 0)),
        pl.BlockSpec((1, D), lambda b, i: (0, 0)),
    ]
    args = [x, g.reshape(1, D)]
    if modulated:
        rm = sc.shape[1]
        if rm == 1:
            mspec = pl.BlockSpec((1, 1, D), lambda b, i: (b, 0, 0))
        else:
            mspec = pl.BlockSpec((1, tt, D), lambda b, i: (b, i, 0))
        in_specs += [mspec, mspec]
        args += [sc, sh]
    out_specs = [pl.BlockSpec((1, tt, D), lambda b, i: (b, i, 0))]
    out_shape = [jax.ShapeDtypeStruct((G, R, D), out_dtype)]
    if transposed:
        out_specs.append(pl.BlockSpec((D, tt), lambda b, i: (0, b * nt + i)))
        out_shape.append(jax.ShapeDtypeStruct((D, G * R), BF16))
    res = pl.pallas_call(
        functools.partial(_norm_kernel, modulated=modulated, transposed=transposed),
        grid=(G, nt),
        in_specs=in_specs,
        out_specs=out_specs,
        out_shape=out_shape,
        compiler_params=_cparams(("arbitrary", "arbitrary")),
        name="norm_mod",
    )(*args)
    return res if transposed else res[0]


def _mm_kernel(*refs, mode, has_bias):
    it = iter(refs)
    h_ref = next(it)
    w_ref = next(it)
    w2_ref = next(it) if mode == "glu" else None
    b_ref = next(it) if has_bias else None
    b2_ref = next(it) if (mode == "glu" and has_bias) else None
    if mode == "resid":
        x_ref = next(it)
        gate_ref = next(it)
    o_ref = next(it)
    ws_ref = next(it)
    ws2_ref = next(it) if mode == "glu" else None

    @pl.when(pl.program_id(1) == 0)
    def _():
        ws_ref[...] = w_ref[...].astype(BF16)
        if mode == "glu":
            ws2_ref[...] = w2_ref[...].astype(BF16)

    h = h_ref[...]
    acc = _dot(h, ws_ref[...])
    if has_bias:
        acc = acc + b_ref[...]
    if mode == "glu":
        gte = _dot(h, ws2_ref[...])
        if has_bias:
            gte = gte + b2_ref[...]
        acc = acc * jax.nn.sigmoid(gte)
    if mode == "resid":
        acc = x_ref[...] + gate_ref[0] * acc
    o_ref[...] = acc.astype(o_ref.dtype)


def matmul(h, w, layer, bias=None, *, mode="plain", n_out=None, xres=None, gate=None,
           rows_per_gate=None, out_dtype=F32, tn=512):
    M, K = h.shape
    nw = w.shape[2]
    N = n_out if n_out is not None else nw
    tm = min(M, 1024)
    nj, ni = N // tn, M // tm
    has_bias = bias is not None
    in_specs = [
        pl.BlockSpec((tm, K), lambda j, i: (i, 0)),
        pl.BlockSpec((None, K, tn), lambda j, i: (layer, 0, j)),
    ]
    args = [h, w]
    if mode == "glu":
        in_specs.append(pl.BlockSpec((None, K, tn), lambda j, i: (layer, 0, j + nj)))
        args.append(w)
    if has_bias:
        b3 = bias.reshape(bias.shape[0], 1, nw)
        in_specs.append(pl.BlockSpec((None, 1, tn), lambda j, i: (layer, 0, j)))
        args.append(b3)
        if mode == "glu":
            in_specs.append(pl.BlockSpec((None, 1, tn), lambda j, i: (layer, 0, j + nj)))
            args.append(b3)
    if mode == "resid":
        in_specs.append(pl.BlockSpec((tm, tn), lambda j, i: (i, j)))
        args.append(xres)
        if gate.shape[1] == 1:
            tiles_per_gate = rows_per_gate // tm
            in_specs.append(pl.BlockSpec((1, 1, tn), lambda j, i: (i // tiles_per_gate, 0, j)))
        else:
            in_specs.append(pl.BlockSpec((1, tm, tn), lambda j, i: (0, i, j)))
        args.append(gate)
    scratch = [pltpu.VMEM((K, tn), BF16)]
    if mode == "glu":
        scratch.append(pltpu.VMEM((K, tn), BF16))
    return pl.pallas_call(
        functools.partial(_mm_kernel, mode=mode, has_bias=has_bias),
        grid=(nj, ni),
        in_specs=in_specs,
        out_specs=pl.BlockSpec((tm, tn), lambda j, i: (i, j)),
        out_shape=jax.ShapeDtypeStruct((M, N), out_dtype),
        scratch_shapes=scratch,
        compiler_params=_cparams(("arbitrary", "arbitrary")),
        name="proj_" + mode,
    )(*args)


def _resid_kernel(x_ref, y_ref, g_ref, o_ref):
    o_ref[0] = x_ref[0] + g_ref[0] * y_ref[0]


def residual(x, y, gate):
    G, R, D = x.shape
    tt = min(R, 512)
    if gate.shape[1] == 1:
        gspec = pl.BlockSpec((1, 1, D), lambda b, i: (b, 0, 0))
    else:
        gspec = pl.BlockSpec((1, tt, D), lambda b, i: (b, i, 0))
    spec = pl.BlockSpec((1, tt, D), lambda b, i: (b, i, 0))
    return pl.pallas_call(
        _resid_kernel,
        grid=(G, R // tt),
        in_specs=[spec, spec, gspec],
        out_specs=spec,
        out_shape=jax.ShapeDtypeStruct((G, R, D), F32),
        compiler_params=_cparams(("arbitrary", "arbitrary")),
        name="residual",
    )(x, y, gate)


def _attn_prompt_kernel(q_ref, kc_ref, kp_ref, vc_ref, vp_ref, bias_ref, sink_ref, o_ref,
                        *, blocks_per_seq):
    first = (pl.program_id(0) % blocks_per_seq) == 0
    col = lax.broadcasted_iota(jnp.int32, (1, 2 * WINDOW), 1)
    pen = jnp.where(jnp.logical_and(first, col < WINDOW), NEG, 0.0).astype(F32)
    for g in range(N_KV_HEADS):
        ks = slice(g * HEAD_DIM, (g + 1) * HEAD_DIM)
        kcat = jnp.concatenate([kp_ref[:, ks], kc_ref[:, ks]], axis=0).astype(BF16)
        vcat = jnp.concatenate([vp_ref[:, ks], vc_ref[:, ks]], axis=0).astype(BF16)
        for hh in range(GROUP):
            h = g * GROUP + hh
            qh = (q_ref[:, h * HEAD_DIM:(h + 1) * HEAD_DIM] * (HEAD_DIM ** -0.5)).astype(BF16)
            s = _dot_nt(qh, kcat) + (bias_ref[h] + pen)
            sink = sink_ref[h]
            m = jnp.maximum(jnp.max(s, axis=-1, keepdims=True), sink)
            p = jnp.exp(s - m)
            den = jnp.sum(p, axis=-1, keepdims=True) + jnp.exp(sink - m)
            o = _dot(p.astype(BF16), vcat) / den
            o_ref[:, h * HEAD_DIM:(h + 1) * HEAD_DIM] = o.astype(o_ref.dtype)


def attn_prompt(qkv, bias_tab, sinks, seq_len):
    M = qkv.shape[0]
    nb = M // WINDOW
    bps = seq_len // WINDOW
    kcol = Q_DIM // KV_DIM
    prev = lambda r: jnp.maximum(r - 1, 0)
    return pl.pallas_call(
        functools.partial(_attn_prompt_kernel, blocks_per_seq=bps),
        grid=(nb,),
        in_specs=[
            pl.BlockSpec((WINDOW, Q_DIM), lambda r: (r, 0)),
            pl.BlockSpec((WINDOW, KV_DIM), lambda r: (r, kcol)),
            pl.BlockSpec((WINDOW, KV_DIM), lambda r: (prev(r), kcol)),
            pl.BlockSpec((WINDOW, KV_DIM), lambda r: (r, kcol + 1)),
            pl.BlockSpec((WINDOW, KV_DIM), lambda r: (prev(r), kcol + 1)),
            pl.BlockSpec((N_HEADS, WINDOW, 2 * WINDOW), lambda r: (0, 0, 0)),
            pl.BlockSpec(memory_space=pltpu.SMEM),
        ],
        out_specs=pl.BlockSpec((WINDOW, Q_DIM), lambda r: (r, 0)),
        out_shape=jax.ShapeDtypeStruct((M, Q_DIM), BF16),
        compiler_params=_cparams(("arbitrary",)),
        name="attn_prompt",
    )(qkv, qkv, qkv, qkv, qkv, bias_tab, sinks)


def _attn_sample_kernel(q_ref, kn_ref, vn_ref, ck_ref, cv_ref, bias_ref, bias0_ref, sink_ref, o_ref):
    for g in range(N_KV_HEADS):
        ks = slice(g * HEAD_DIM, (g + 1) * HEAD_DIM)
        hs = slice(g * GROUP, (g + 1) * GROUP)
        qg = q_ref[0, hs, :] * (HEAD_DIM ** -0.5)
        kn = kn_ref[0, :, ks]
        vn = vn_ref[0, :, ks]
        s = _dot_nt(qg.astype(BF16), ck_ref[0, :, ks].astype(BF16)) + bias_ref[hs, :]
        s_new = jnp.sum(qg * kn, axis=-1, keepdims=True) + bias0_ref[hs, :]
        sink = sink_ref[hs, :]
        m = jnp.maximum(jnp.maximum(jnp.max(s, axis=-1, keepdims=True), s_new), sink)
        p = jnp.exp(s - m)
        pn = jnp.exp(s_new - m)
        den = jnp.sum(p, axis=-1, keepdims=True) + pn + jnp.exp(sink - m)
        o = (_dot(p.astype(BF16), cv_ref[0, :, ks].astype(BF16)) + pn * vn) / den
        o_ref[0, hs, :] = o.astype(o_ref.dtype)


def attn_sample(q3, k_new, v_new, cache_k, cache_v, bias_s, bias0, sinks):
    nseq = cache_k.shape[0]
    return pl.pallas_call(
        _attn_sample_kernel,
        grid=(nseq,),
        in_specs=[
            pl.BlockSpec((1, N_HEADS, HEAD_DIM), lambda b: (b, 0, 0)),
            pl.BlockSpec((1, 1, KV_DIM), lambda b: (b, 0, 0)),
            pl.BlockSpec((1, 1, KV_DIM), lambda b: (b, 0, 0)),
            pl.BlockSpec((1, WINDOW, KV_DIM), lambda b: (b, 0, 0)),
            pl.BlockSpec((1, WINDOW, KV_DIM), lambda b: (b, 0, 0)),
            pl.BlockSpec((N_HEADS, WINDOW), lambda b: (0, 0)),
            pl.BlockSpec((N_HEADS, 1), lambda b: (0, 0)),
            pl.BlockSpec((N_HEADS, 1), lambda b: (0, 0)),
        ],
        out_specs=pl.BlockSpec((1, N_HEADS, HEAD_DIM), lambda b: (b, 0, 0)),
        out_shape=jax.ShapeDtypeStruct((nseq, N_HEADS, HEAD_DIM), F32),
        compiler_params=_cparams(("arbitrary",)),
        name="attn_sample",
    )(q3, k_new, v_new, cache_k, cache_v, bias_s, bias0, sinks)


CONV_HALO = 32


def _conv_prompt_kernel(u_ref, halo_ref, w_ref, b_ref, o_ref, scr_ref, *, tt):
    first = pl.program_id(1) == 0

    @pl.when(first)
    def _():
        scr_ref[0:CONV_HALO, :] = jnp.zeros((CONV_HALO, scr_ref.shape[1]), F32)

    @pl.when(jnp.logical_not(first))
    def _():
        scr_ref[0:CONV_HALO, :] = halo_ref[0]

    scr_ref[CONV_HALO:, :] = u_ref[0]
    off = CONV_HALO - (CONV_WIDTH - 1)
    acc = jnp.broadcast_to(b_ref[...], (tt, scr_ref.shape[1]))
    for w in range(CONV_WIDTH):
        acc = acc + scr_ref[off + w:off + w + tt, :] * w_ref[w:w + 1, :]
    o_ref[0] = acc


def conv_prompt(u, w_dw, b_dw, layer):
    B, T, D = u.shape
    tt, dc = 512, 512
    hb = tt // CONV_HALO
    return pl.pallas_call(
        functools.partial(_conv_prompt_kernel, tt=tt),
        grid=(B, T // tt, D // dc),
        in_specs=[
            pl.BlockSpec((1, tt, dc), lambda b, i, j: (b, i, j)),
            pl.BlockSpec((1, CONV_HALO, dc), lambda b, i, j: (b, jnp.maximum(i * hb - 1, 0), j)),
            pl.BlockSpec((None, CONV_WIDTH, dc), lambda b, i, j: (layer, 0, j)),
            pl.BlockSpec((None, 1, dc), lambda b, i, j: (layer, 0, j)),
        ],
        out_specs=pl.BlockSpec((1, tt, dc), lambda b, i, j: (b, i, j)),
        out_shape=jax.ShapeDtypeStruct((B, T, D), F32),
        scratch_shapes=[pltpu.VMEM((CONV_HALO + tt, dc), F32)],
        compiler_params=_cparams(("arbitrary", "arbitrary", "arbitrary")),
        name="conv_prompt",
    )(u, u, w_dw, b_dw.reshape(b_dw.shape[0], 1, D))


def _conv_sample_kernel(st_ref, u_ref, w_ref, b_ref, o_ref):
    acc = u_ref[...] * w_ref[CONV_WIDTH - 1:CONV_WIDTH, :] + b_ref[...]
    for w in range(CONV_WIDTH - 1):
        acc = acc + st_ref[w] * w_ref[w:w + 1, :]
    o_ref[...] = acc


def conv_sample(state_t, u, w_dw, b_dw, layer):
    nseq, D = u.shape
    return pl.pallas_call(
        _conv_sample_kernel,
        grid=(1,),
        in_specs=[
            pl.BlockSpec((CONV_WIDTH - 1, nseq, D), lambda i: (0, 0, 0)),
            pl.BlockSpec((nseq, D), lambda i: (0, 0)),
            pl.BlockSpec((None, CONV_WIDTH, D), lambda i: (layer, 0, 0)),
            pl.BlockSpec((None, 1, D), lambda i: (layer, 0, 0)),
        ],
        out_specs=pl.BlockSpec((nseq, D), lambda i: (0, 0)),
        out_shape=jax.ShapeDtypeStruct((nseq, D), F32),
        compiler_params=_cparams(("arbitrary",)),
        name="conv_sample",
    )(state_t, u, w_dw, b_dw.reshape(b_dw.shape[0], 1, D))


def _ln_swish_kernel(y_ref, g_ref, b_ref, o_ref):
    y = y_ref[...]
    mu = jnp.mean(y, axis=-1, keepdims=True)
    yc = y - mu
    var = jnp.mean(yc * yc, axis=-1, keepdims=True)
    z = yc * lax.rsqrt(var + EPS) * g_ref[...] + b_ref[...]
    o_ref[...] = (z * jax.nn.sigmoid(z)).astype(o_ref.dtype)


def ln_swish(y, ln_g, ln_b, layer):
    M, D = y.shape
    tt = min(M, 512)
    return pl.pallas_call(
        _ln_swish_kernel,
        grid=(M // tt,),
        in_specs=[
            pl.BlockSpec((tt, D), lambda i: (i, 0)),
            pl.BlockSpec((None, 1, D), lambda i: (layer, 0, 0)),
            pl.BlockSpec((None, 1, D), lambda i: (layer, 0, 0)),
        ],
        out_specs=pl.BlockSpec((tt, D), lambda i: (i, 0)),
        out_shape=jax.ShapeDtypeStruct((M, D), BF16),
        compiler_params=_cparams(("arbitrary",)),
        name="ln_swish",
    )(y, ln_g.reshape(-1, 1, D), ln_b.reshape(-1, 1, D))


def _top_values(s, k):
    out = []
    for _ in range(k):
        m = jnp.max(s, axis=0, keepdims=True)
        out.append(m)
        s = jnp.where(s == m, -jnp.inf, s)
    return out


def _peer_topk_kernel(q_ref, keys_ref, s_ref, tau_ref, c_ref, top_ref):
    for h in range(PEER_HEADS):
        for p in range(2):
            hp = 2 * h + p
            qs = q_ref[:, hp * D_KEY_HALF:(hp + 1) * D_KEY_HALF].astype(BF16)
            s = _dot_nt(keys_ref[hp].astype(BF16), qs)
            s_ref[hp] = s
            for k, m in enumerate(_top_values(s, PEER_TOPK)):
                top_ref[p, k:k + 1, :] = m
        t1 = top_ref[1]
        pieces = [top_ref[0, 0:1, :] + t1]
        for a in range(1, PEER_TOPK):
            pieces.append(top_ref[0, a:a + 1, :] + t1[0:8])
        best = _top_values(jnp.concatenate(pieces, axis=0), PEER_TOPK)
        m = best[0]
        z = jnp.exp(best[0] - m)
        for v in best[1:]:
            z = z + jnp.exp(v - m)
        tau_ref[h:h + 1, :] = best[-1]
        c_ref[h:h + 1, :] = m + jnp.log(z)


def peer_topk(q, sub_keys, layer):
    M = q.shape[0]
    tt = min(M, 256)
    keys = sub_keys.reshape(DEPTH, PEER_HEADS * 2, N_KEYS, D_KEY_HALF)
    return pl.pallas_call(
        _peer_topk_kernel,
        grid=(M // tt,),
        in_specs=[
            pl.BlockSpec((tt, PEER_HEADS * 2 * D_KEY_HALF), lambda i: (i, 0)),
            pl.BlockSpec((None, PEER_HEADS * 2, N_KEYS, D_KEY_HALF), lambda i: (layer, 0, 0, 0)),
        ],
        out_specs=[
            pl.BlockSpec((PEER_HEADS * 2, N_KEYS, tt), lambda i: (0, 0, i)),
            pl.BlockSpec((PEER_HEADS, tt), lambda i: (0, i)),
            pl.BlockSpec((PEER_HEADS, tt), lambda i: (0, i)),
        ],
        out_shape=[
            jax.ShapeDtypeStruct((PEER_HEADS * 2, N_KEYS, M), F32),
            jax.ShapeDtypeStruct((PEER_HEADS, M), F32),
            jax.ShapeDtypeStruct((PEER_HEADS, M), F32),
        ],
        scratch_shapes=[pltpu.VMEM((2, PEER_TOPK, tt), F32)],
        compiler_params=_cparams(("arbitrary",)),
        name="peer_topk",
    )(q, keys)


PEER_EC = 512


def _peer_dense_kernel(hT_ref, s_ref, tau_ref, c_ref, u_ref, v_ref, o_ref):
    e = pl.program_id(1)

    @pl.when(e == 0)
    def _():
        o_ref[...] = jnp.zeros(o_ref.shape, F32)

    act = _dot(u_ref[...].astype(BF16), hT_ref[...])
    ge = 0.5 * act * (1.0 + lax.erf(act * (1.0 / math.sqrt(2.0))))
    parts = []
    for ii in range(PEER_EC // N_KEYS):
        i = e * (PEER_EC // N_KEYS) + ii
        g = None
        for h in range(PEER_HEADS):
            row0 = s_ref[2 * h, pl.ds(i, 1), :]
            tot = s_ref[2 * h + 1] + row0
            val = jnp.where(tot >= tau_ref[h:h + 1, :], jnp.exp(tot - c_ref[h:h + 1, :]), 0.0)
            g = val if g is None else g + val
        parts.append(g)
    wT = (ge * jnp.concatenate(parts, axis=0)).astype(BF16)
    o_ref[...] += _dot_tn(wT, v_ref[...].astype(BF16))


def peer_dense(hT, sT, tau, c, u_tab, v_tab, layer):
    D, M = hT.shape
    tt = min(M, 512)
    return pl.pallas_call(
        _peer_dense_kernel,
        grid=(M // tt, N_EXPERTS // PEER_EC),
        in_specs=[
            pl.BlockSpec((D, tt), lambda t, e: (0, t)),
            pl.BlockSpec((PEER_HEADS * 2, N_KEYS, tt), lambda t, e: (0, 0, t)),
            pl.BlockSpec((PEER_HEADS, tt), lambda t, e: (0, t)),
            pl.BlockSpec((PEER_HEADS, tt), lambda t, e: (0, t)),
            pl.BlockSpec((None, PEER_EC, D), lambda t, e: (layer, e, 0)),
            pl.BlockSpec((None, PEER_EC, D), lambda t, e: (layer, e, 0)),
        ],
        out_specs=pl.BlockSpec((tt, D), lambda t, e: (t, 0)),
        out_shape=jax.ShapeDtypeStruct((M, D), F32),
        compiler_params=_cparams(("arbitrary", "arbitrary")),
        name="peer_dense",
    )(hT, sT, tau, c, u_tab, v_tab)


def _t5_bucket(rel):
    n = jnp.maximum(rel, 0)
    max_exact = NUM_BUCKETS // 2
    nf = jnp.maximum(n, 1).astype(F32)
    large = max_exact + (jnp.log(nf / max_exact) / math.log(MAX_DISTANCE / max_exact)
                         * (NUM_BUCKETS - max_exact)).astype(jnp.int32)
    large = jnp.minimum(large, NUM_BUCKETS - 1)
    return jnp.where(n < max_exact, n, large)


def _bias_tables(rel_bias):
    rb = rel_bias.astype(F32)
    rel = jnp.arange(WINDOW, dtype=jnp.int32)[:, None] - (jnp.arange(2 * WINDOW, dtype=jnp.int32) - WINDOW)[None, :]
    valid = (rel >= 0) & (rel < WINDOW)
    tab = jnp.where(valid[:, :, None], rb[_t5_bucket(rel)], NEG)
    bias_p = tab.transpose(2, 0, 1)
    rel_s = WINDOW - jnp.arange(WINDOW, dtype=jnp.int32)
    tab_s = jnp.where((rel_s < WINDOW)[:, None], rb[_t5_bucket(rel_s)], NEG)
    bias_s = tab_s.T
    bias0 = rb[_t5_bucket(jnp.zeros((1,), jnp.int32))].T
    return bias_p, bias_s, bias0


def _peer_layer(x, g_ffn_i, sc, sh, gt, w_pq, sub_keys, u_tab, v_tab, i):
    G, R, D = x.shape
    h, hT = norm_mod(x, g_ffn_i, sc, sh, transposed=True)
    q = matmul(h.reshape(G * R, D), w_pq, i)
    sT, tau, c = peer_topk(q, sub_keys, i)
    y = peer_dense(hT, sT, tau, c, u_tab, v_tab, i)
    return residual(x, y.reshape(G, R, D), gt)


def _split_mod(mod):
    return [mod[:, :, k * D_MODEL:(k + 1) * D_MODEL] for k in range(6)]


def kernel(x_prompt, x_sample, cache_k, cache_v, state_conv, c_prompt, c_sample, rel_bias,
           w_ada, b_ada, g_mix, g_ffn, g_final, w_qkv, b_qkv, w_o, sinks,
           w_pw1, b_pw1, w_dw, b_dw, ln_g, ln_b, w_pw2, b_pw2, w_pq, sub_keys, u_tab, v_tab):
    B, T, D = x_prompt.shape
    S = x_sample.shape[0]
    n_attn = w_qkv.shape[0]

    rows = B + S
    rows_pad = -(-rows // 16) * 16
    c_all = jnp.concatenate([c_prompt, c_sample, jnp.zeros((rows_pad - rows, D), F32)], axis=0)
    mod = ada_mod(c_all, w_ada, b_ada)
    bias_p, bias_s, bias0 = _bias_tables(rel_bias)

    xp = x_prompt
    xs = jnp.pad(x_sample.reshape(1, S, D), ((0, 0), (0, SAMPLE_PAD - S), (0, 0)))
    new_kp, new_vp, new_cp, new_ks, new_vs, new_cs = [], [], [], [], [], []
    tm_rows = T

    for i in range(DEPTH):
        mp = _split_mod(mod[i, :B].reshape(B, 1, 6 * D))
        ms = _split_mod(jnp.pad(mod[i, B:B + S], ((0, SAMPLE_PAD - S), (0, 0))).reshape(1, SAMPLE_PAD, 6 * D))
        hp = norm_mod(xp, g_mix[i], mp[1], mp[0]).reshape(B * T, D)
        hs = norm_mod(xs, g_mix[i], ms[1], ms[0]).reshape(SAMPLE_PAD, D)
        if i % 2 == 0:
            a = i // 2
            sink_a = sinks[a].astype(F32)
            qkv = matmul(hp, w_qkv, a, b_qkv)
            o = attn_prompt(qkv, bias_p, sink_a, T)
            xp = matmul(o, w_o, a, mode="resid", xres=xp.reshape(B * T, D), gate=mp[2],
                        rows_per_gate=tm_rows).reshape(B, T, D)
            kv = qkv.reshape(B, T, Q_DIM + 2 * KV_DIM)[:, T - WINDOW:, Q_DIM:]
            new_kp.append(kv[..., :KV_DIM].reshape(B, WINDOW, N_KV_HEADS, HEAD_DIM))
            new_vp.append(kv[..., KV_DIM:].reshape(B, WINDOW, N_KV_HEADS, HEAD_DIM))
            qkv_s = matmul(hs, w_qkv, a, b_qkv)
            q3 = qkv_s[:S, :Q_DIM].reshape(S, N_HEADS, HEAD_DIM)
            k_new = qkv_s[:S, Q_DIM:Q_DIM + KV_DIM]
            v_new = qkv_s[:S, Q_DIM + KV_DIM:]
            ck = cache_k[a].reshape(S, WINDOW, KV_DIM)
            cv = cache_v[a].reshape(S, WINDOW, KV_DIM)
            o_s = attn_sample(q3, k_new.reshape(S, 1, KV_DIM), v_new.reshape(S, 1, KV_DIM), ck, cv,
                              bias_s, bias0, sink_a.reshape(N_HEADS, 1))
            o_s = jnp.pad(o_s.reshape(S, Q_DIM), ((0, SAMPLE_PAD - S), (0, 0))).astype(BF16)
            xs = matmul(o_s, w_o, a, mode="resid",
                        xres=xs.reshape(SAMPLE_PAD, D), gate=ms[2]).reshape(1, SAMPLE_PAD, D)
            new_ks.append(jnp.concatenate([ck[:, 1:], k_new[:, None]], axis=1)
                          .reshape(S, WINDOW, N_KV_HEADS, HEAD_DIM))
            new_vs.append(jnp.concatenate([cv[:, 1:], v_new[:, None]], axis=1)
                          .reshape(S, WINDOW, N_KV_HEADS, HEAD_DIM))
        else:
            bl = i // 2
            u = matmul(hp, w_pw1, bl, b_pw1, mode="glu", n_out=D).reshape(B, T, D)
            y = conv_prompt(u, w_dw, b_dw, bl)
            z = ln_swish(y.reshape(B * T, D), ln_g, ln_b, bl)
            xp = matmul(z, w_pw2, bl, b_pw2, mode="resid", xres=xp.reshape(B * T, D), gate=mp[2],
                        rows_per_gate=tm_rows).reshape(B, T, D)
            new_cp.append(u[:, T - (CONV_WIDTH - 1):])
            u_s = matmul(hs, w_pw1, bl, b_pw1, mode="glu", n_out=D)
            st = state_conv[bl]
            y_s = conv_sample(st.transpose(1, 0, 2), u_s[:S], w_dw, b_dw, bl)
            z_s = ln_swish(jnp.pad(y_s, ((0, SAMPLE_PAD - S), (0, 0))), ln_g, ln_b, bl)
            xs = matmul(z_s, w_pw2, bl, b_pw2, mode="resid", xres=xs.reshape(SAMPLE_PAD, D),
                        gate=ms[2]).reshape(1, SAMPLE_PAD, D)
            new_cs.append(jnp.concatenate([st[:, 1:], u_s[:S, None]], axis=1))
        xp = _peer_layer(xp, g_ffn[i], mp[4], mp[3], mp[5], w_pq, sub_keys, u_tab, v_tab, i)
        xs = _peer_layer(xs, g_ffn[i], ms[4], ms[3], ms[5], w_pq, sub_keys, u_tab, v_tab, i)

    y_prompt = norm_mod(xp, g_final, out_dtype=F32)
    y_sample = norm_mod(xs, g_final, out_dtype=F32)[0, :S].reshape(S, 1, D)
    return (y_prompt, y_sample, jnp.stack(new_kp), jnp.stack(new_vp), jnp.stack(new_cp),
            jnp.stack(new_ks), jnp.stack(new_vs), jnp.stack(new_cs))
```

```python
import functools
import math

import jax
import jax.numpy as jnp
import numpy as np
from jax import lax
from jax.experimental import pallas as pl
from jax.experimental.pallas import tpu as pltpu

D_MODEL = 2048
DEPTH = 4
N_HEADS = 64
N_KV_HEADS = 8
HEAD_DIM = 64
GROUP = N_HEADS // N_KV_HEADS
Q_DIM = N_HEADS * HEAD_DIM
KV_DIM = N_KV_HEADS * HEAD_DIM
WINDOW = 128
NUM_BUCKETS = 32
MAX_DISTANCE = 128
CONV_WIDTH = 31
PEER_HEADS = 8
N_KEYS = 128
N_EXPERTS = N_KEYS * N_KEYS
PEER_TOPK = 16
D_KEY_HALF = 128
EPS = 1e-6
NEG = -1e30

VMEM_LIMIT_V7X = 56 * 1024 * 1024
SAMPLE_PAD = 128

BF16 = jnp.bfloat16
F32 = jnp.float32


def _cparams(sem):
    return pltpu.CompilerParams(dimension_semantics=sem, vmem_limit_bytes=VMEM_LIMIT_V7X)


def _dot(a, b):
    return jnp.dot(a, b, preferred_element_type=F32)


def _dot_nt(a, b):
    return lax.dot_general(a, b, (((1,), (1,)), ((), ())), preferred_element_type=F32)


def _dot_tn(a, b):
    return lax.dot_general(a, b, (((0,), (0,)), ((), ())), preferred_element_type=F32)


def _ada_kernel(c_ref, w_ref, b_ref, o_ref):
    c = c_ref[...]
    cond = (c * jax.nn.sigmoid(c)).astype(BF16)
    o_ref[...] = _dot(cond, w_ref[...].astype(BF16)) + b_ref[...]


def ada_mod(c_all, w_ada, b_ada):
    rows = c_all.shape[0]
    n = w_ada.shape[2]
    tn = 1536
    return pl.pallas_call(
        _ada_kernel,
        grid=(DEPTH, n // tn),
        in_specs=[
            pl.BlockSpec((rows, D_MODEL), lambda l, j: (0, 0)),
            pl.BlockSpec((None, D_MODEL, tn), lambda l, j: (l, 0, j)),
            pl.BlockSpec((None, 1, tn), lambda l, j: (l, 0, j)),
        ],
        out_specs=pl.BlockSpec((None, rows, tn), lambda l, j: (l, 0, j)),
        out_shape=jax.ShapeDtypeStruct((DEPTH, rows, n), F32),
        compiler_params=_cparams(("arbitrary", "arbitrary")),
        name="ada_mod",
    )(c_all, w_ada, b_ada.reshape(DEPTH, 1, n))


def _norm_kernel(*refs, modulated, transposed):
    if modulated:
        x_ref, g_ref, sc_ref, sh_ref = refs[:4]
        outs = refs[4:]
    else:
        x_ref, g_ref = refs[:2]
        outs = refs[2:]
    x = x_ref[0]
    y = x * lax.rsqrt(jnp.mean(x * x, axis=-1, keepdims=True) + EPS) * g_ref[...]
    if modulated:
        y = y * (1.0 + sc_ref[0]) + sh_ref[0]
    outs[0][0] = y.astype(outs[0].dtype)
    if transposed:
        outs[1][...] = y.T.astype(outs[1].dtype)


def norm_mod(x, g, sc=None, sh=None, *, out_dtype=BF16, transposed=False):
    G, R, D = x.shape
    tt = min(R, 512)
    nt = R // tt
    modulated = sc is not None
    in_specs = [
        pl.BlockSpec((1, tt, D), lambda b, i: (b, i, 0)),
        pl.BlockSpec((1, D), lambda b, i: (0, 0)),
    ]
    args = [x, g.reshape(1, D)]
    if modulated:
        rm = sc.shape[1]
        if rm == 1:
            mspec = pl.BlockSpec((1, 1, D), lambda b, i: (b, 0, 0))
        else:
            mspec = pl.BlockSpec((1, tt, D), lambda b, i: (b, i, 0))
        in_specs += [mspec, mspec]
        args += [sc, sh]
    out_specs = [pl.BlockSpec((1, tt, D), lambda b, i: (b, i, 0))]
    out_shape = [jax.ShapeDtypeStruct((G, R, D), out_dtype)]
    if transposed:
        out_specs.append(pl.BlockSpec((D, tt), lambda b, i: (0, b * nt + i)))
        out_shape.append(jax.ShapeDtypeStruct((D, G * R), BF16))
    res = pl.pallas_call(
        functools.partial(_norm_kernel, modulated=modulated, transposed=transposed),
        grid=(G, nt),
        in_specs=in_specs,
        out_specs=out_specs,
        out_shape=out_shape,
        compiler_params=_cparams(("arbitrary", "arbitrary")),
        name="norm_mod",
    )(*args)
    return res if transposed else res[0]


def _mm_kernel(*refs, mode, has_bias):
    it = iter(refs)
    h_ref = next(it)
    w_ref = next(it)
    w2_ref = next(it) if mode == "glu" else None
    b_ref = next(it) if has_bias else None
    b2_ref = next(it) if (mode == "glu" and has_bias) else None
    if mode == "resid":
        x_ref = next(it)
        gate_ref = next(it)
    o_ref = next(it)
    ws_ref = next(it)
    ws2_ref = next(it) if mode == "glu" else None

    @pl.when(pl.program_id(1) == 0)
    def _():
        ws_ref[...] = w_ref[...].astype(BF16)
        if mode == "glu":
            ws2_ref[...] = w2_ref[...].astype(BF16)

    h = h_ref[...]
    acc = _dot(h, ws_ref[...])
    if has_bias:
        acc = acc + b_ref[...]
    if mode == "glu":
        gte = _dot(h, ws2_ref[...])
        if has_bias:
            gte = gte + b2_ref[...]
        acc = acc * jax.nn.sigmoid(gte)
    if mode == "resid":
        acc = x_ref[...] + gate_ref[0] * acc
    o_ref[...] = acc.astype(o_ref.dtype)


def matmul(h, w, layer, bias=None, *, mode="plain", n_out=None, xres=None, gate=None,
           rows_per_gate=None, out_dtype=F32, tn=512):
    M, K = h.shape
    nw = w.shape[2]
    N = n_out if n_out is not None else nw
    tm = min(M, 1024)
    nj, ni = N // tn, M // tm
    has_bias = bias is not None
    in_specs = [
        pl.BlockSpec((tm, K), lambda j, i: (i, 0)),
        pl.BlockSpec((None, K, tn), lambda j, i: (layer, 0, j)),
    ]
    args = [h, w]
    if mode == "glu":
        in_specs.append(pl.BlockSpec((None, K, tn), lambda j, i: (layer, 0, j + nj)))
        args.append(w)
    if has_bias:
        b3 = bias.reshape(bias.shape[0], 1, nw)
        in_specs.append(pl.BlockSpec((None, 1, tn), lambda j, i: (layer, 0, j)))
        args.append(b3)
        if mode == "glu":
            in_specs.append(pl.BlockSpec((None, 1, tn), lambda j, i: (layer, 0, j + nj)))
            args.append(b3)
    if mode == "resid":
        in_specs.append(pl.BlockSpec((tm, tn), lambda j, i: (i, j)))
        args.append(xres)
        if gate.shape[1] == 1:
            tiles_per_gate = rows_per_gate // tm
            in_specs.append(pl.BlockSpec((1, 1, tn), lambda j, i: (i // tiles_per_gate, 0, j)))
        else:
            in_specs.append(pl.BlockSpec((1, tm, tn), lambda j, i: (0, i, j)))
        args.append(gate)
    scratch = [pltpu.VMEM((K, tn), BF16)]
    if mode == "glu":
        scratch.append(pltpu.VMEM((K, tn), BF16))
    return pl.pallas_call(
        functools.partial(_mm_kernel, mode=mode, has_bias=has_bias),
        grid=(nj, ni),
        in_specs=in_specs,
        out_specs=pl.BlockSpec((tm, tn), lambda j, i: (i, j)),
        out_shape=jax.ShapeDtypeStruct((M, N), out_dtype),
        scratch_shapes=scratch,
        compiler_params=_cparams(("arbitrary", "arbitrary")),
        name="proj_" + mode,
    )(*args)


def _resid_kernel(x_ref, y_ref, g_ref, o_ref):
    o_ref[0] = x_ref[0] + g_ref[0] * y_ref[0]


def residual(x, y, gate):
    G, R, D = x.shape
    tt = min(R, 512)
    if gate.shape[1] == 1:
        gspec = pl.BlockSpec((1, 1, D), lambda b, i: (b, 0, 0))
    else:
        gspec = pl.BlockSpec((1, tt, D), lambda b, i: (b, i, 0))
    spec = pl.BlockSpec((1, tt, D), lambda b, i: (b, i, 0))
    return pl.pallas_call(
        _resid_kernel,
        grid=(G, R // tt),
        in_specs=[spec, spec, gspec],
        out_specs=spec,
        out_shape=jax.ShapeDtypeStruct((G, R, D), F32),
        compiler_params=_cparams(("arbitrary", "arbitrary")),
        name="residual",
    )(x, y, gate)


def _attn_prompt_kernel(q_ref, kc_ref, kp_ref, vc_ref, vp_ref, bias_ref, sink_ref, o_ref,
                        *, blocks_per_seq):
    first = (pl.program_id(0) % blocks_per_seq) == 0
    col = lax.broadcasted_iota(jnp.int32, (1, 2 * WINDOW), 1)
    pen = jnp.where(jnp.logical_and(first, col < WINDOW), NEG, 0.0).astype(F32)
    for g in range(N_KV_HEADS):
        ks = slice(g * HEAD_DIM, (g + 1) * HEAD_DIM)
        kcat = jnp.concatenate([kp_ref[:, ks], kc_ref[:, ks]], axis=0).astype(BF16)
        vcat = jnp.concatenate([vp_ref[:, ks], vc_ref[:, ks]], axis=0).astype(BF16)
        for hh in range(GROUP):
            h = g * GROUP + hh
            qh = (q_ref[:, h * HEAD_DIM:(h + 1) * HEAD_DIM] * (HEAD_DIM ** -0.5)).astype(BF16)
            s = _dot_nt(qh, kcat) + (bias_ref[h] + pen)
            sink = sink_ref[h]
            m = jnp.maximum(jnp.max(s, axis=-1, keepdims=True), sink)
            p = jnp.exp(s - m)
            den = jnp.sum(p, axis=-1, keepdims=True) + jnp.exp(sink - m)
            o = _dot(p.astype(BF16), vcat) / den
            o_ref[:, h * HEAD_DIM:(h + 1) * HEAD_DIM] = o.astype(o_ref.dtype)


def attn_prompt(qkv, bias_tab, sinks, seq_len):
    M = qkv.shape[0]
    nb = M // WINDOW
    bps = seq_len // WINDOW
    kcol = Q_DIM // KV_DIM
    prev = lambda r: jnp.maximum(r - 1, 0)
    return pl.pallas_call(
        functools.partial(_attn_prompt_kernel, blocks_per_seq=bps),
        grid=(nb,),
        in_specs=[
            pl.BlockSpec((WINDOW, Q_DIM), lambda r: (r, 0)),
            pl.BlockSpec((WINDOW, KV_DIM), lambda r: (r, kcol)),
            pl.BlockSpec((WINDOW, KV_DIM), lambda r: (prev(r), kcol)),
            pl.BlockSpec((WINDOW, KV_DIM), lambda r: (r, kcol + 1)),
            pl.BlockSpec((WINDOW, KV_DIM), lambda r: (prev(r), kcol + 1)),
            pl.BlockSpec((N_HEADS, WINDOW, 2 * WINDOW), lambda r: (0, 0, 0)),
            pl.BlockSpec(memory_space=pltpu.SMEM),
        ],
        out_specs=pl.BlockSpec((WINDOW, Q_DIM), lambda r: (r, 0)),
        out_shape=jax.ShapeDtypeStruct((M, Q_DIM), BF16),
        compiler_params=_cparams(("arbitrary",)),
        name="attn_prompt",
    )(qkv, qkv, qkv, qkv, qkv, bias_tab, sinks)


def _attn_sample_kernel(q_ref, kn_ref, vn_ref, ck_ref, cv_ref, bias_ref, bias0_ref, sink_ref, o_ref):
    for g in range(N_KV_HEADS):
        ks = slice(g * HEAD_DIM, (g + 1) * HEAD_DIM)
        hs = slice(g * GROUP, (g + 1) * GROUP)
        qg = q_ref[0, hs, :] * (HEAD_DIM ** -0.5)
        kn = kn_ref[0, :, ks]
        vn = vn_ref[0, :, ks]
        s = _dot_nt(qg.astype(BF16), ck_ref[0, :, ks].astype(BF16)) + bias_ref[hs, :]
        s_new = jnp.sum(qg * kn, axis=-1, keepdims=True) + bias0_ref[hs, :]
        sink = sink_ref[hs, :]
        m = jnp.maximum(jnp.maximum(jnp.max(s, axis=-1, keepdims=True), s_new), sink)
        p = jnp.exp(s - m)
        pn = jnp.exp(s_new - m)
        den = jnp.sum(p, axis=-1, keepdims=True) + pn + jnp.exp(sink - m)
        o = (_dot(p.astype(BF16), cv_ref[0, :, ks].astype(BF16)) + pn * vn) / den
        o_ref[0, hs, :] = o.astype(o_ref.dtype)


def attn_sample(q3, k_new, v_new, cache_k, cache_v, bias_s, bias0, sinks):
    nseq = cache_k.shape[0]
    return pl.pallas_call(
        _attn_sample_kernel,
        grid=(nseq,),
        in_specs=[
            pl.BlockSpec((1, N_HEADS, HEAD_DIM), lambda b: (b, 0, 0)),
            pl.BlockSpec((1, 1, KV_DIM), lambda b: (b, 0, 0)),
            pl.BlockSpec((1, 1, KV_DIM), lambda b: (b, 0, 0)),
            pl.BlockSpec((1, WINDOW, KV_DIM), lambda b: (b, 0, 0)),
            pl.BlockSpec((1, WINDOW, KV_DIM), lambda b: (b, 0, 0)),
            pl.BlockSpec((N_HEADS, WINDOW), lambda b: (0, 0)),
            pl.BlockSpec((N_HEADS, 1), lambda b: (0, 0)),
            pl.BlockSpec((N_HEADS, 1), lambda b: (0, 0)),
        ],
        out_specs=pl.BlockSpec((1, N_HEADS, HEAD_DIM), lambda b: (b, 0, 0)),
        out_shape=jax.ShapeDtypeStruct((nseq, N_HEADS, HEAD_DIM), F32),
        compiler_params=_cparams(("arbitrary",)),
        name="attn_sample",
    )(q3, k_new, v_new, cache_k, cache_v, bias_s, bias0, sinks)


CONV_HALO = 32


def _conv_prompt_kernel(u_ref, halo_ref, w_ref, b_ref, o_ref, scr_ref, *, tt):
    first = pl.program_id(1) == 0

    @pl.when(first)
    def _():
        scr_ref[0:CONV_HALO, :] = jnp.zeros((CONV_HALO, scr_ref.shape[1]), F32)

    @pl.when(jnp.logical_not(first))
    def _():
        scr_ref[0:CONV_HALO, :] = halo_ref[0]

    scr_ref[CONV_HALO:, :] = u_ref[0]
    off = CONV_HALO - (CONV_WIDTH - 1)
    acc = jnp.broadcast_to(b_ref[...], (tt, scr_ref.shape[1]))
    for w in range(CONV_WIDTH):
        acc = acc + scr_ref[off + w:off + w + tt, :] * w_ref[w:w + 1, :]
    o_ref[0] = acc


def conv_prompt(u, w_dw, b_dw, layer):
    B, T, D = u.shape
    tt, dc = 512, 512
    hb = tt // CONV_HALO
    return pl.pallas_call(
        functools.partial(_conv_prompt_kernel, tt=tt),
        grid=(B, T // tt, D // dc),
        in_specs=[
            pl.BlockSpec((1, tt, dc), lambda b, i, j: (b, i, j)),
            pl.BlockSpec((1, CONV_HALO, dc), lambda b, i, j: (b, jnp.maximum(i * hb - 1, 0), j)),
            pl.BlockSpec((None, CONV_WIDTH, dc), lambda b, i, j: (layer, 0, j)),
            pl.BlockSpec((None, 1, dc), lambda b, i, j: (layer, 0, j)),
        ],
        out_specs=pl.BlockSpec((1, tt, dc), lambda b, i, j: (b, i, j)),
        out_shape=jax.ShapeDtypeStruct((B, T, D), F32),
        scratch_shapes=[pltpu.VMEM((CONV_HALO + tt, dc), F32)],
        compiler_params=_cparams(("arbitrary", "arbitrary", "arbitrary")),
        name="conv_prompt",
    )(u, u, w_dw, b_dw.reshape(b_dw.shape[0], 1, D))


def _conv_sample_kernel(st_ref, u_ref, w_ref, b_ref, o_ref):
    acc = u_ref[...] * w_ref[CONV_WIDTH - 1:CONV_WIDTH, :] + b_ref[...]
    for w in range(CONV_WIDTH - 1):
        acc = acc + st_ref[w] * w_ref[w:w + 1, :]
    o_ref[...] = acc


def conv_sample(state_t, u, w_dw, b_dw, layer):
    nseq, D = u.shape
    return pl.pallas_call(
        _conv_sample_kernel,
        grid=(1,),
        in_specs=[
            pl.BlockSpec((CONV_WIDTH - 1, nseq, D), lambda i: (0, 0, 0)),
            pl.BlockSpec((nseq, D), lambda i: (0, 0)),
            pl.BlockSpec((None, CONV_WIDTH, D), lambda i: (layer, 0, 0)),
            pl.BlockSpec((None, 1, D), lambda i: (layer, 0, 0)),
        ],
        out_specs=pl.BlockSpec((nseq, D), lambda i: (0, 0)),
        out_shape=jax.ShapeDtypeStruct((nseq, D), F32),
        compiler_params=_cparams(("arbitrary",)),
        name="conv_sample",
    )(state_t, u, w_dw, b_dw.reshape(b_dw.shape[0], 1, D))


def _ln_swish_kernel(y_ref, g_ref, b_ref, o_ref):
    y = y_ref[...]
    mu = jnp.mean(y, axis=-1, keepdims=True)
    yc = y - mu
    var = jnp.mean(yc * yc, axis=-1, keepdims=True)
    z = yc * lax.rsqrt(var + EPS) * g_ref[...] + b_ref[...]
    o_ref[...] = (z * jax.nn.sigmoid(z)).astype(o_ref.dtype)


def ln_swish(y, ln_g, ln_b, layer):
    M, D = y.shape
    tt = min(M, 512)
    return pl.pallas_call(
        _ln_swish_kernel,
        grid=(M // tt,),
        in_specs=[
            pl.BlockSpec((tt, D), lambda i: (i, 0)),
            pl.BlockSpec((None, 1, D), lambda i: (layer, 0, 0)),
            pl.BlockSpec((None, 1, D), lambda i: (layer, 0, 0)),
        ],
        out_specs=pl.BlockSpec((tt, D), lambda i: (i, 0)),
        out_shape=jax.ShapeDtypeStruct((M, D), BF16),
        compiler_params=_cparams(("arbitrary",)),
        name="ln_swish",
    )(y, ln_g.reshape(-1, 1, D), ln_b.reshape(-1, 1, D))


NOT_RANKED = 127.0


def _top_values(s, k, with_rank=False):
    out = []
    rank = jnp.full(s.shape, NOT_RANKED, F32) if with_rank else None
    for i in range(k):
        m = jnp.max(s, axis=0, keepdims=True)
        out.append(m)
        hit = s == m
        if with_rank:
            rank = jnp.where(hit, float(i), rank)
        s = jnp.where(hit, -jnp.inf, s)
    return out, rank


def _peer_topk_kernel(q_ref, keys_ref, n_ref, a_ref, r1_ref, e1_ref, top_ref):
    for h in range(PEER_HEADS):
        halves = []
        for p in range(2):
            hp = 2 * h + p
            qs = q_ref[:, hp * D_KEY_HALF:(hp + 1) * D_KEY_HALF].astype(BF16)
            s = _dot_nt(keys_ref[hp].astype(BF16), qs)
            tops, rank = _top_values(s, PEER_TOPK, with_rank=(p == 1))
            for k, m in enumerate(tops):
                top_ref[p, k:k + 1, :] = m
            halves.append((s, tops, rank))
        (s0, t0, _), (s1, t1, rank1) = halves
        t1_all = top_ref[1]
        pieces = [t0[0] + t1_all] + [t0[a] + t1_all[0:8] for a in range(1, PEER_TOPK)]
        best, _ = _top_values(jnp.concatenate(pieces, axis=0), PEER_TOPK)
        tau = best[-1]
        z = jnp.exp(best[0] - best[0])
        for v in best[1:]:
            z = z + jnp.exp(v - best[0])
        n = jnp.zeros(s0.shape, F32)
        for a in range(PEER_TOPK):
            cnt = jnp.sum(jnp.where(t0[a] + t1_all >= tau, 1.0, 0.0), axis=0, keepdims=True)
            n = jnp.where(s0 == t0[a], cnt, n)
        n_ref[h] = n
        a_ref[h] = jnp.exp(s0 - t0[0])
        r1_ref[h] = rank1.astype(BF16)
        e1_ref[h] = (jnp.exp(s1 - t1[0]) / z).astype(BF16)


def peer_topk(q, sub_keys, layer):
    M = q.shape[0]
    tt = min(M, 256)
    keys = sub_keys.reshape(DEPTH, PEER_HEADS * 2, N_KEYS, D_KEY_HALF)
    return pl.pallas_call(
        _peer_topk_kernel,
        grid=(M // tt,),
        in_specs=[
            pl.BlockSpec((tt, PEER_HEADS * 2 * D_KEY_HALF), lambda i: (i, 0)),
            pl.BlockSpec((None, PEER_HEADS * 2, N_KEYS, D_KEY_HALF), lambda i: (layer, 0, 0, 0)),
        ],
        out_specs=[pl.BlockSpec((PEER_HEADS, N_KEYS, tt), lambda i: (0, 0, i))] * 4,
        out_shape=[
            jax.ShapeDtypeStruct((PEER_HEADS, N_KEYS, M), F32),
            jax.ShapeDtypeStruct((PEER_HEADS, N_KEYS, M), F32),
            jax.ShapeDtypeStruct((PEER_HEADS, N_KEYS, M), BF16),
            jax.ShapeDtypeStruct((PEER_HEADS, N_KEYS, M), BF16),
        ],
        scratch_shapes=[pltpu.VMEM((2, PEER_TOPK, tt), F32)],
        compiler_params=_cparams(("arbitrary",)),
        name="peer_topk",
    )(q, keys)


PEER_EC = 512


BF16_SUBLANES = 16


def _peer_dense_kernel(hT_ref, n_ref, a_ref, r1_ref, e1_ref, u_ref, v_ref, o_ref):
    e = pl.program_id(1)
    tt = hT_ref.shape[1]

    @pl.when(e == 0)
    def _():
        o_ref[...] = jnp.zeros(o_ref.shape, F32)

    act = _dot(u_ref[...].astype(BF16), hT_ref[...])
    ge = (0.5 * act * (1.0 + lax.erf(act * (1.0 / math.sqrt(2.0))))).astype(BF16)
    parts = []
    for ii in range(PEER_EC // N_KEYS):
        i = e * (PEER_EC // N_KEYS) + ii
        g = None
        for h in range(PEER_HEADS):
            nrow = jnp.broadcast_to(n_ref[h, pl.ds(i, 1), :], (BF16_SUBLANES, tt)).astype(BF16)
            arow = jnp.broadcast_to(a_ref[h, pl.ds(i, 1), :], (BF16_SUBLANES, tt)).astype(BF16)
            gate = e1_ref[h] * arow[None]
            val = jnp.where(r1_ref[h] < nrow[None], gate, jnp.zeros_like(gate))
            g = val if g is None else g + val
        parts.append(g.reshape(N_KEYS, tt))
    wT = ge * jnp.concatenate(parts, axis=0)
    o_ref[...] += _dot_tn(wT, v_ref[...].astype(BF16))


def peer_dense(hT, n, a, r1, e1, u_tab, v_tab, layer):
    D, M = hT.shape
    tt = min(M, 1024)
    groups = N_KEYS // BF16_SUBLANES
    r1 = r1.reshape(PEER_HEADS, groups, BF16_SUBLANES, M)
    e1 = e1.reshape(PEER_HEADS, groups, BF16_SUBLANES, M)
    once = pl.Buffered(1)
    return pl.pallas_call(
        _peer_dense_kernel,
        grid=(M // tt, N_EXPERTS // PEER_EC),
        in_specs=[
            pl.BlockSpec((D, tt), lambda t, e: (0, t), pipeline_mode=once),
            pl.BlockSpec((PEER_HEADS, N_KEYS, tt), lambda t, e: (0, 0, t), pipeline_mode=once),
            pl.BlockSpec((PEER_HEADS, N_KEYS, tt), lambda t, e: (0, 0, t), pipeline_mode=once),
            pl.BlockSpec((PEER_HEADS, groups, BF16_SUBLANES, tt), lambda t, e: (0, 0, 0, t), pipeline_mode=once),
            pl.BlockSpec((PEER_HEADS, groups, BF16_SUBLANES, tt), lambda t, e: (0, 0, 0, t), pipeline_mode=once),
            pl.BlockSpec((None, PEER_EC, D), lambda t, e: (layer, e, 0)),
            pl.BlockSpec((None, PEER_EC, D), lambda t, e: (layer, e, 0)),
        ],
        out_specs=pl.BlockSpec((tt, D), lambda t, e: (t, 0), pipeline_mode=once),
        out_shape=jax.ShapeDtypeStruct((M, D), F32),
        compiler_params=_cparams(("arbitrary", "arbitrary")),
        name="peer_dense",
    )(hT, n, a, r1, e1, u_tab, v_tab)


def _t5_bucket(rel):
    n = jnp.maximum(rel, 0)
    max_exact = NUM_BUCKETS // 2
    nf = jnp.maximum(n, 1).astype(F32)
    large = max_exact + (jnp.log(nf / max_exact) / math.log(MAX_DISTANCE / max_exact)
                         * (NUM_BUCKETS - max_exact)).astype(jnp.int32)
    large = jnp.minimum(large, NUM_BUCKETS - 1)
    return jnp.where(n < max_exact, n, large)


def _bias_tables(rel_bias):
    rb = rel_bias.astype(F32)
    rel = jnp.arange(WINDOW, dtype=jnp.int32)[:, None] - (jnp.arange(2 * WINDOW, dtype=jnp.int32) - WINDOW)[None, :]
    valid = (rel >= 0) & (rel < WINDOW)
    tab = jnp.where(valid[:, :, None], rb[_t5_bucket(rel)], NEG)
    bias_p = tab.transpose(2, 0, 1)
    rel_s = WINDOW - jnp.arange(WINDOW, dtype=jnp.int32)
    tab_s = jnp.where((rel_s < WINDOW)[:, None], rb[_t5_bucket(rel_s)], NEG)
    bias_s = tab_s.T
    bias0 = rb[_t5_bucket(jnp.zeros((1,), jnp.int32))].T
    return bias_p, bias_s, bias0


def _peer_layer(x, g_ffn_i, sc, sh, gt, w_pq, sub_keys, u_tab, v_tab, i):
    G, R, D = x.shape
    h, hT = norm_mod(x, g_ffn_i, sc, sh, transposed=True)
    q = matmul(h.reshape(G * R, D), w_pq, i)
    n, a, r1, e1 = peer_topk(q, sub_keys, i)
    y = peer_dense(hT, n, a, r1, e1, u_tab, v_tab, i)
    return residual(x, y.reshape(G, R, D), gt)


def _split_mod(mod):
    return [mod[:, :, k * D_MODEL:(k + 1) * D_MODEL] for k in range(6)]


def kernel(x_prompt, x_sample, cache_k, cache_v, state_conv, c_prompt, c_sample, rel_bias,
           w_ada, b_ada, g_mix, g_ffn, g_final, w_qkv, b_qkv, w_o, sinks,
           w_pw1, b_pw1, w_dw, b_dw, ln_g, ln_b, w_pw2, b_pw2, w_pq, sub_keys, u_tab, v_tab):
    B, T, D = x_prompt.shape
    S = x_sample.shape[0]
    n_attn = w_qkv.shape[0]

    rows = B + S
    rows_pad = -(-rows // 16) * 16
    c_all = jnp.concatenate([c_prompt, c_sample, jnp.zeros((rows_pad - rows, D), F32)], axis=0)
    mod = ada_mod(c_all, w_ada, b_ada)
    bias_p, bias_s, bias0 = _bias_tables(rel_bias)

    xp = x_prompt
    xs = jnp.pad(x_sample.reshape(1, S, D), ((0, 0), (0, SAMPLE_PAD - S), (0, 0)))
    new_kp, new_vp, new_cp, new_ks, new_vs, new_cs = [], [], [], [], [], []
    tm_rows = T

    for i in range(DEPTH):
        mp = _split_mod(mod[i, :B].reshape(B, 1, 6 * D))
        ms = _split_mod(jnp.pad(mod[i, B:B + S], ((0, SAMPLE_PAD - S), (0, 0))).reshape(1, SAMPLE_PAD, 6 * D))
        hp = norm_mod(xp, g_mix[i], mp[1], mp[0]).reshape(B * T, D)
        hs = norm_mod(xs, g_mix[i], ms[1], ms[0]).reshape(SAMPLE_PAD, D)
        if i % 2 == 0:
            a = i // 2
            sink_a = sinks[a].astype(F32)
            qkv = matmul(hp, w_qkv, a, b_qkv)
            o = attn_prompt(qkv, bias_p, sink_a, T)
            xp = matmul(o, w_o, a, mode="resid", xres=xp.reshape(B * T, D), gate=mp[2],
                        rows_per_gate=tm_rows).reshape(B, T, D)
            kv = qkv.reshape(B, T, Q_DIM + 2 * KV_DIM)[:, T - WINDOW:, Q_DIM:]
            new_kp.append(kv[..., :KV_DIM].reshape(B, WINDOW, N_KV_HEADS, HEAD_DIM))
            new_vp.append(kv[..., KV_DIM:].reshape(B, WINDOW, N_KV_HEADS, HEAD_DIM))
            qkv_s = matmul(hs, w_qkv, a, b_qkv)
            q3 = qkv_s[:S, :Q_DIM].reshape(S, N_HEADS, HEAD_DIM)
            k_new = qkv_s[:S, Q_DIM:Q_DIM + KV_DIM]
            v_new = qkv_s[:S, Q_DIM + KV_DIM:]
            ck = cache_k[a].reshape(S, WINDOW, KV_DIM)
            cv = cache_v[a].reshape(S, WINDOW, KV_DIM)
            o_s = attn_sample(q3, k_new.reshape(S, 1, KV_DIM), v_new.reshape(S, 1, KV_DIM), ck, cv,
                              bias_s, bias0, sink_a.reshape(N_HEADS, 1))
            o_s = jnp.pad(o_s.reshape(S, Q_DIM), ((0, SAMPLE_PAD - S), (0, 0))).astype(BF16)
            xs = matmul(o_s, w_o, a, mode="resid",
                        xres=xs.reshape(SAMPLE_PAD, D), gate=ms[2]).reshape(1, SAMPLE_PAD, D)
            new_ks.append(jnp.concatenate([ck[:, 1:], k_new[:, None]], axis=1)
                          .reshape(S, WINDOW, N_KV_HEADS, HEAD_DIM))
            new_vs.append(jnp.concatenate([cv[:, 1:], v_new[:, None]], axis=1)
                          .reshape(S, WINDOW, N_KV_HEADS, HEAD_DIM))
        else:
            bl = i // 2
            u = matmul(hp, w_pw1, bl, b_pw1, mode="glu", n_out=D).reshape(B, T, D)
            y = conv_prompt(u, w_dw, b_dw, bl)
            z = ln_swish(y.reshape(B * T, D), ln_g, ln_b, bl)
            xp = matmul(z, w_pw2, bl, b_pw2, mode="resid", xres=xp.reshape(B * T, D), gate=mp[2],
                        rows_per_gate=tm_rows).reshape(B, T, D)
            new_cp.append(u[:, T - (CONV_WIDTH - 1):])
            u_s = matmul(hs, w_pw1, bl, b_pw1, mode="glu", n_out=D)
            st = state_conv[bl]
            y_s = conv_sample(st.transpose(1, 0, 2), u_s[:S], w_dw, b_dw, bl)
            z_s = ln_swish(jnp.pad(y_s, ((0, SAMPLE_PAD - S), (0, 0))), ln_g, ln_b, bl)
            xs = matmul(z_s, w_pw2, bl, b_pw2, mode="resid", xres=xs.reshape(SAMPLE_PAD, D),
                        gate=ms[2]).reshape(1, SAMPLE_PAD, D)
            new_cs.append(jnp.concatenate([st[:, 1:], u_s[:S, None]], axis=1))
        xp = _peer_layer(xp, g_ffn[i], mp[4], mp[3], mp[5], w_pq, sub_keys, u_tab, v_tab, i)
        xs = _peer_layer(xs, g_ffn[i], ms[4], ms[3], ms[5], w_pq, sub_keys, u_tab, v_tab, i)

    y_prompt = norm_mod(xp, g_final, out_dtype=F32)
    y_sample = norm_mod(xs, g_final, out_dtype=F32)[0, :S].reshape(S, 1, D)
    return (y_prompt, y_sample, jnp.stack(new_kp), jnp.stack(new_vp), jnp.stack(new_cp),
            jnp.stack(new_ks), jnp.stack(new_vs), jnp.stack(new_cs))
```

```python
import functools
import math

import jax
import jax.numpy as jnp
import numpy as np
from jax import lax
from jax.experimental import pallas as pl
from jax.experimental.pallas import tpu as pltpu

D_MODEL = 2048
DEPTH = 4
N_HEADS = 64
N_KV_HEADS = 8
HEAD_DIM = 64
GROUP = N_HEADS // N_KV_HEADS
Q_DIM = N_HEADS * HEAD_DIM
KV_DIM = N_KV_HEADS * HEAD_DIM
WINDOW = 128
NUM_BUCKETS = 32
MAX_DISTANCE = 128
CONV_WIDTH = 31
PEER_HEADS = 8
N_KEYS = 128
N_EXPERTS = N_KEYS * N_KEYS
PEER_TOPK = 16
D_KEY_HALF = 128
EPS = 1e-6
NEG = -1e30

VMEM_LIMIT_V7X = 56 * 1024 * 1024
SAMPLE_PAD = 128

BF16 = jnp.bfloat16
F32 = jnp.float32


def _cparams(sem, flags=None):
    return pltpu.CompilerParams(dimension_semantics=sem, vmem_limit_bytes=VMEM_LIMIT_V7X, flags=flags)


def _dot(a, b):
    return jnp.dot(a, b, preferred_element_type=F32)


def _dot_nt(a, b):
    return lax.dot_general(a, b, (((1,), (1,)), ((), ())), preferred_element_type=F32)


def _dot_tn(a, b):
    return lax.dot_general(a, b, (((0,), (0,)), ((), ())), preferred_element_type=F32)


def _ada_kernel(c_ref, w_ref, b_ref, o_ref):
    c = c_ref[...]
    cond = (c * jax.nn.sigmoid(c)).astype(BF16)
    o_ref[...] = _dot(cond, w_ref[...].astype(BF16)) + b_ref[...]


def ada_mod(c_all, w_ada, b_ada):
    rows = c_all.shape[0]
    n = w_ada.shape[2]
    tn = 1536
    return pl.pallas_call(
        _ada_kernel,
        grid=(DEPTH, n // tn),
        in_specs=[
            pl.BlockSpec((rows, D_MODEL), lambda l, j: (0, 0)),
            pl.BlockSpec((None, D_MODEL, tn), lambda l, j: (l, 0, j)),
            pl.BlockSpec((None, 1, tn), lambda l, j: (l, 0, j)),
        ],
        out_specs=pl.BlockSpec((None, rows, tn), lambda l, j: (l, 0, j)),
        out_shape=jax.ShapeDtypeStruct((DEPTH, rows, n), F32),
        compiler_params=_cparams(("arbitrary", "arbitrary")),
        name="ada_mod",
    )(c_all, w_ada, b_ada.reshape(DEPTH, 1, n))


def _norm_kernel(*refs, modulated, transposed):
    if modulated:
        x_ref, g_ref, sc_ref, sh_ref = refs[:4]
        outs = refs[4:]
    else:
        x_ref, g_ref = refs[:2]
        outs = refs[2:]
    x = x_ref[0]
    y = x * lax.rsqrt(jnp.mean(x * x, axis=-1, keepdims=True) + EPS) * g_ref[...]
    if modulated:
        y = y * (1.0 + sc_ref[0]) + sh_ref[0]
    outs[0][0] = y.astype(outs[0].dtype)
    if transposed:
        outs[1][...] = y.T.astype(outs[1].dtype)


def norm_mod(x, g, sc=None, sh=None, *, out_dtype=BF16, transposed=False):
    G, R, D = x.shape
    tt = min(R, 512)
    nt = R // tt
    modulated = sc is not None
    in_specs = [
        pl.BlockSpec((1, tt, D), lambda b, i: (b, i, 0)),
        pl.BlockSpec((1, D), lambda b, i: (0, 0)),
    ]
    args = [x, g.reshape(1, D)]
    if modulated:
        rm = sc.shape[1]
        if rm == 1:
            mspec = pl.BlockSpec((1, 1, D), lambda b, i: (b, 0, 0))
        else:
            mspec = pl.BlockSpec((1, tt, D), lambda b, i: (b, i, 0))
        in_specs += [mspec, mspec]
        args += [sc, sh]
    out_specs = [pl.BlockSpec((1, tt, D), lambda b, i: (b, i, 0))]
    out_shape = [jax.ShapeDtypeStruct((G, R, D), out_dtype)]
    if transposed:
        out_specs.append(pl.BlockSpec((D, tt), lambda b, i: (0, b * nt + i)))
        out_shape.append(jax.ShapeDtypeStruct((D, G * R), BF16))
    res = pl.pallas_call(
        functools.partial(_norm_kernel, modulated=modulated, transposed=transposed),
        grid=(G, nt),
        in_specs=in_specs,
        out_specs=out_specs,
        out_shape=out_shape,
        compiler_params=_cparams(("arbitrary", "arbitrary")),
        name="norm_mod",
    )(*args)
    return res if transposed else res[0]


def _mm_kernel(*refs, mode, has_bias):
    it = iter(refs)
    h_ref = next(it)
    w_ref = next(it)
    w2_ref = next(it) if mode == "glu" else None
    b_ref = next(it) if has_bias else None
    b2_ref = next(it) if (mode == "glu" and has_bias) else None
    if mode == "resid":
        x_ref = next(it)
        gate_ref = next(it)
    o_ref = next(it)
    ws_ref = next(it)
    ws2_ref = next(it) if mode == "glu" else None

    @pl.when(pl.program_id(1) == 0)
    def _():
        ws_ref[...] = w_ref[...].astype(BF16)
        if mode == "glu":
            ws2_ref[...] = w2_ref[...].astype(BF16)

    h = h_ref[...]
    acc = _dot(h, ws_ref[...])
    if has_bias:
        acc = acc + b_ref[...]
    if mode == "glu":
        gte = _dot(h, ws2_ref[...])
        if has_bias:
            gte = gte + b2_ref[...]
        acc = acc * jax.nn.sigmoid(gte)
    if mode == "resid":
        acc = x_ref[...] + gate_ref[0] * acc
    o_ref[...] = acc.astype(o_ref.dtype)


def matmul(h, w, layer, bias=None, *, mode="plain", n_out=None, xres=None, gate=None,
           rows_per_gate=None, out_dtype=F32, tn=512):
    M, K = h.shape
    nw = w.shape[2]
    N = n_out if n_out is not None else nw
    tm = min(M, 1024)
    nj, ni = N // tn, M // tm
    has_bias = bias is not None
    in_specs = [
        pl.BlockSpec((tm, K), lambda j, i: (i, 0)),
        pl.BlockSpec((None, K, tn), lambda j, i: (layer, 0, j)),
    ]
    args = [h, w]
    if mode == "glu":
        in_specs.append(pl.BlockSpec((None, K, tn), lambda j, i: (layer, 0, j + nj)))
        args.append(w)
    if has_bias:
        b3 = bias.reshape(bias.shape[0], 1, nw)
        in_specs.append(pl.BlockSpec((None, 1, tn), lambda j, i: (layer, 0, j)))
        args.append(b3)
        if mode == "glu":
            in_specs.append(pl.BlockSpec((None, 1, tn), lambda j, i: (layer, 0, j + nj)))
            args.append(b3)
    if mode == "resid":
        in_specs.append(pl.BlockSpec((tm, tn), lambda j, i: (i, j)))
        args.append(xres)
        if gate.shape[1] == 1:
            tiles_per_gate = rows_per_gate // tm
            in_specs.append(pl.BlockSpec((1, 1, tn), lambda j, i: (i // tiles_per_gate, 0, j)))
        else:
            in_specs.append(pl.BlockSpec((1, tm, tn), lambda j, i: (0, i, j)))
        args.append(gate)
    scratch = [pltpu.VMEM((K, tn), BF16)]
    if mode == "glu":
        scratch.append(pltpu.VMEM((K, tn), BF16))
    return pl.pallas_call(
        functools.partial(_mm_kernel, mode=mode, has_bias=has_bias),
        grid=(nj, ni),
        in_specs=in_specs,
        out_specs=pl.BlockSpec((tm, tn), lambda j, i: (i, j)),
        out_shape=jax.ShapeDtypeStruct((M, N), out_dtype),
        scratch_shapes=scratch,
        compiler_params=_cparams(("arbitrary", "arbitrary")),
        name="proj_" + mode,
    )(*args)


def _resid_kernel(x_ref, y_ref, g_ref, o_ref):
    o_ref[0] = x_ref[0] + g_ref[0] * y_ref[0]


def residual(x, y, gate):
    G, R, D = x.shape
    tt = min(R, 512)
    if gate.shape[1] == 1:
        gspec = pl.BlockSpec((1, 1, D), lambda b, i: (b, 0, 0))
    else:
        gspec = pl.BlockSpec((1, tt, D), lambda b, i: (b, i, 0))
    spec = pl.BlockSpec((1, tt, D), lambda b, i: (b, i, 0))
    return pl.pallas_call(
        _resid_kernel,
        grid=(G, R // tt),
        in_specs=[spec, spec, gspec],
        out_specs=spec,
        out_shape=jax.ShapeDtypeStruct((G, R, D), F32),
        compiler_params=_cparams(("arbitrary", "arbitrary")),
        name="residual",
    )(x, y, gate)


PAIRS = GROUP // 2


def _attn_prompt_kernel(q_ref, kc_ref, kp_ref, vc_ref, vp_ref, bias_ref, sink_ref, o_ref,
                        *, blocks_per_seq):
    first = (pl.program_id(0) % blocks_per_seq) == 0
    col = lax.broadcasted_iota(jnp.int32, (1, 2 * WINDOW), 1)
    pen = jnp.where(jnp.logical_and(first, col < WINDOW), NEG, 0.0).astype(F32)
    zeros = jnp.zeros((2 * WINDOW, HEAD_DIM), BF16)
    rows = PAIRS * WINDOW
    for g in range(N_KV_HEADS):
        ks = slice(g * HEAD_DIM, (g + 1) * HEAD_DIM)
        kcat = jnp.concatenate([kp_ref[:, ks], kc_ref[:, ks]], axis=0).astype(BF16)
        vcat = jnp.concatenate([vp_ref[:, ks], vc_ref[:, ks]], axis=0).astype(BF16)
        qs = slice(g * GROUP * HEAD_DIM, (g + 1) * GROUP * HEAD_DIM)
        qg = jnp.concatenate(
            [q_ref[:, qs.start + pp * 2 * HEAD_DIM:qs.start + (pp + 1) * 2 * HEAD_DIM] for pp in range(PAIRS)],
            axis=0)
        qg = (qg * (HEAD_DIM ** -0.5)).astype(BF16)
        out = None
        for par in range(2):
            kx = jnp.concatenate([kcat, zeros] if par == 0 else [zeros, kcat], axis=1)
            vx = jnp.concatenate([vcat, zeros] if par == 0 else [zeros, vcat], axis=1)
            s = _dot_nt(qg, kx) + (bias_ref[g, par] + pen)
            sink = jnp.concatenate(
                [jnp.full((WINDOW, 1), sink_ref[g * GROUP + 2 * pp + par], F32) for pp in range(PAIRS)], axis=0)
            m = jnp.maximum(jnp.max(s, axis=-1, keepdims=True), sink)
            p = jnp.exp(s - m)
            den = jnp.sum(p, axis=-1, keepdims=True) + jnp.exp(sink - m)
            o = _dot(p.astype(BF16), vx) / den
            out = o if out is None else out + o
        for pp in range(PAIRS):
            o_ref[:, qs.start + pp * 2 * HEAD_DIM:qs.start + (pp + 1) * 2 * HEAD_DIM] = (
                out[pp * WINDOW:(pp + 1) * WINDOW].astype(o_ref.dtype))


def attn_prompt(qkv, bias_tab, sinks, seq_len):
    M = qkv.shape[0]
    nb = M // WINDOW
    bps = seq_len // WINDOW
    kcol = Q_DIM // KV_DIM
    prev = lambda r: jnp.maximum(r - 1, 0)
    return pl.pallas_call(
        functools.partial(_attn_prompt_kernel, blocks_per_seq=bps),
        grid=(nb,),
        in_specs=[
            pl.BlockSpec((WINDOW, Q_DIM), lambda r: (r, 0)),
            pl.BlockSpec((WINDOW, KV_DIM), lambda r: (r, kcol)),
            pl.BlockSpec((WINDOW, KV_DIM), lambda r: (prev(r), kcol)),
            pl.BlockSpec((WINDOW, KV_DIM), lambda r: (r, kcol + 1)),
            pl.BlockSpec((WINDOW, KV_DIM), lambda r: (prev(r), kcol + 1)),
            pl.BlockSpec((N_KV_HEADS, 2, PAIRS * WINDOW, 2 * WINDOW), lambda r: (0, 0, 0, 0),
                         pipeline_mode=pl.Buffered(1)),
            pl.BlockSpec(memory_space=pltpu.SMEM),
        ],
        out_specs=pl.BlockSpec((WINDOW, Q_DIM), lambda r: (r, 0)),
        out_shape=jax.ShapeDtypeStruct((M, Q_DIM), BF16),
        compiler_params=_cparams(("arbitrary",)),
        name="attn_prompt",
    )(qkv, qkv, qkv, qkv, qkv, bias_tab, sinks)


def _attn_sample_kernel(q_ref, kn_ref, vn_ref, ck_ref, cv_ref, bias_ref, bias0_ref, sink_ref, o_ref):
    for g in range(N_KV_HEADS):
        ks = slice(g * HEAD_DIM, (g + 1) * HEAD_DIM)
        hs = slice(g * GROUP, (g + 1) * GROUP)
        qg = q_ref[0, hs, :] * (HEAD_DIM ** -0.5)
        kn = kn_ref[0, :, ks]
        vn = vn_ref[0, :, ks]
        s = _dot_nt(qg.astype(BF16), ck_ref[0, :, ks].astype(BF16)) + bias_ref[hs, :]
        s_new = jnp.sum(qg * kn, axis=-1, keepdims=True) + bias0_ref[hs, :]
        sink = sink_ref[hs, :]
        m = jnp.maximum(jnp.maximum(jnp.max(s, axis=-1, keepdims=True), s_new), sink)
        p = jnp.exp(s - m)
        pn = jnp.exp(s_new - m)
        den = jnp.sum(p, axis=-1, keepdims=True) + pn + jnp.exp(sink - m)
        o = (_dot(p.astype(BF16), cv_ref[0, :, ks].astype(BF16)) + pn * vn) / den
        o_ref[0, hs, :] = o.astype(o_ref.dtype)


def attn_sample(q3, k_new, v_new, cache_k, cache_v, bias_s, bias0, sinks):
    nseq = cache_k.shape[0]
    return pl.pallas_call(
        _attn_sample_kernel,
        grid=(nseq,),
        in_specs=[
            pl.BlockSpec((1, N_HEADS, HEAD_DIM), lambda b: (b, 0, 0)),
            pl.BlockSpec((1, 1, KV_DIM), lambda b: (b, 0, 0)),
            pl.BlockSpec((1, 1, KV_DIM), lambda b: (b, 0, 0)),
            pl.BlockSpec((1, WINDOW, KV_DIM), lambda b: (b, 0, 0)),
            pl.BlockSpec((1, WINDOW, KV_DIM), lambda b: (b, 0, 0)),
            pl.BlockSpec((N_HEADS, WINDOW), lambda b: (0, 0)),
            pl.BlockSpec((N_HEADS, 1), lambda b: (0, 0)),
            pl.BlockSpec((N_HEADS, 1), lambda b: (0, 0)),
        ],
        out_specs=pl.BlockSpec((1, N_HEADS, HEAD_DIM), lambda b: (b, 0, 0)),
        out_shape=jax.ShapeDtypeStruct((nseq, N_HEADS, HEAD_DIM), F32),
        compiler_params=_cparams(("arbitrary",)),
        name="attn_sample",
    )(q3, k_new, v_new, cache_k, cache_v, bias_s, bias0, sinks)


CONV_HALO = 32


def _conv_prompt_kernel(u_ref, halo_ref, w_ref, b_ref, o_ref, scr_ref, *, tt):
    first = pl.program_id(1) == 0

    @pl.when(first)
    def _():
        scr_ref[0:CONV_HALO, :] = jnp.zeros((CONV_HALO, scr_ref.shape[1]), F32)

    @pl.when(jnp.logical_not(first))
    def _():
        scr_ref[0:CONV_HALO, :] = halo_ref[0]

    scr_ref[CONV_HALO:, :] = u_ref[0]
    off = CONV_HALO - (CONV_WIDTH - 1)
    acc = jnp.broadcast_to(b_ref[...], (tt, scr_ref.shape[1]))
    for w in range(CONV_WIDTH):
        acc = acc + scr_ref[off + w:off + w + tt, :] * w_ref[w:w + 1, :]
    o_ref[0] = acc


def conv_prompt(u, w_dw, b_dw, layer):
    B, T, D = u.shape
    tt, dc = 512, 512
    hb = tt // CONV_HALO
    return pl.pallas_call(
        functools.partial(_conv_prompt_kernel, tt=tt),
        grid=(B, T // tt, D // dc),
        in_specs=[
            pl.BlockSpec((1, tt, dc), lambda b, i, j: (b, i, j)),
            pl.BlockSpec((1, CONV_HALO, dc), lambda b, i, j: (b, jnp.maximum(i * hb - 1, 0), j)),
            pl.BlockSpec((None, CONV_WIDTH, dc), lambda b, i, j: (layer, 0, j)),
            pl.BlockSpec((None, 1, dc), lambda b, i, j: (layer, 0, j)),
        ],
        out_specs=pl.BlockSpec((1, tt, dc), lambda b, i, j: (b, i, j)),
        out_shape=jax.ShapeDtypeStruct((B, T, D), F32),
        scratch_shapes=[pltpu.VMEM((CONV_HALO + tt, dc), F32)],
        compiler_params=_cparams(("arbitrary", "arbitrary", "arbitrary")),
        name="conv_prompt",
    )(u, u, w_dw, b_dw.reshape(b_dw.shape[0], 1, D))


def _conv_sample_kernel(st_ref, u_ref, w_ref, b_ref, o_ref):
    acc = u_ref[...] * w_ref[CONV_WIDTH - 1:CONV_WIDTH, :] + b_ref[...]
    for w in range(CONV_WIDTH - 1):
        acc = acc + st_ref[w] * w_ref[w:w + 1, :]
    o_ref[...] = acc


def conv_sample(state_t, u, w_dw, b_dw, layer):
    nseq, D = u.shape
    return pl.pallas_call(
        _conv_sample_kernel,
        grid=(1,),
        in_specs=[
            pl.BlockSpec((CONV_WIDTH - 1, nseq, D), lambda i: (0, 0, 0)),
            pl.BlockSpec((nseq, D), lambda i: (0, 0)),
            pl.BlockSpec((None, CONV_WIDTH, D), lambda i: (layer, 0, 0)),
            pl.BlockSpec((None, 1, D), lambda i: (layer, 0, 0)),
        ],
        out_specs=pl.BlockSpec((nseq, D), lambda i: (0, 0)),
        out_shape=jax.ShapeDtypeStruct((nseq, D), F32),
        compiler_params=_cparams(("arbitrary",)),
        name="conv_sample",
    )(state_t, u, w_dw, b_dw.reshape(b_dw.shape[0], 1, D))


def _ln_swish_kernel(y_ref, g_ref, b_ref, o_ref):
    y = y_ref[...]
    mu = jnp.mean(y, axis=-1, keepdims=True)
    yc = y - mu
    var = jnp.mean(yc * yc, axis=-1, keepdims=True)
    z = yc * lax.rsqrt(var + EPS) * g_ref[...] + b_ref[...]
    o_ref[...] = (z * jax.nn.sigmoid(z)).astype(o_ref.dtype)


def ln_swish(y, ln_g, ln_b, layer):
    M, D = y.shape
    tt = min(M, 512)
    return pl.pallas_call(
        _ln_swish_kernel,
        grid=(M // tt,),
        in_specs=[
            pl.BlockSpec((tt, D), lambda i: (i, 0)),
            pl.BlockSpec((None, 1, D), lambda i: (layer, 0, 0)),
            pl.BlockSpec((None, 1, D), lambda i: (layer, 0, 0)),
        ],
        out_specs=pl.BlockSpec((tt, D), lambda i: (i, 0)),
        out_shape=jax.ShapeDtypeStruct((M, D), BF16),
        compiler_params=_cparams(("arbitrary",)),
        name="ln_swish",
    )(y, ln_g.reshape(-1, 1, D), ln_b.reshape(-1, 1, D))


NOT_RANKED = 127.0


def _top_values(s, k, with_rank=False):
    out = []
    rank = jnp.full(s.shape, NOT_RANKED, F32) if with_rank else None
    for i in range(k):
        m = jnp.max(s, axis=0, keepdims=True)
        out.append(m)
        hit = s == m
        if with_rank:
            rank = jnp.where(hit, float(i), rank)
        s = jnp.where(hit, -jnp.inf, s)
    return out, rank


def _peer_topk_kernel(q_ref, keys_ref, n_ref, a_ref, r1_ref, e1_ref, top_ref):
    for h in range(PEER_HEADS):
        halves = []
        for p in range(2):
            hp = 2 * h + p
            qs = q_ref[:, hp * D_KEY_HALF:(hp + 1) * D_KEY_HALF].astype(BF16)
            s = _dot_nt(keys_ref[hp].astype(BF16), qs)
            tops, rank = _top_values(s, PEER_TOPK, with_rank=(p == 1))
            for k, m in enumerate(tops):
                top_ref[p, k:k + 1, :] = m
            halves.append((s, tops, rank))
        (s0, t0, _), (s1, t1, rank1) = halves
        t1_all = top_ref[1]
        pieces = [t0[0] + t1_all] + [t0[a] + t1_all[0:8] for a in range(1, PEER_TOPK)]
        best, _ = _top_values(jnp.concatenate(pieces, axis=0), PEER_TOPK)
        tau = best[-1]
        z = jnp.exp(best[0] - best[0])
        for v in best[1:]:
            z = z + jnp.exp(v - best[0])
        n = jnp.zeros(s0.shape, F32)
        for a in range(PEER_TOPK):
            cnt = jnp.sum(jnp.where(t0[a] + t1_all >= tau, 1.0, 0.0), axis=0, keepdims=True)
            n = jnp.where(s0 == t0[a], cnt, n)
        n_ref[h] = n
        a_ref[h] = jnp.exp(s0 - t0[0])
        r1_ref[h] = rank1.astype(BF16)
        e1_ref[h] = (jnp.exp(s1 - t1[0]) / z).astype(BF16)


def peer_topk(q, sub_keys, layer):
    M = q.shape[0]
    tt = min(M, 256)
    keys = sub_keys.reshape(DEPTH, PEER_HEADS * 2, N_KEYS, D_KEY_HALF)
    return pl.pallas_call(
        _peer_topk_kernel,
        grid=(M // tt,),
        in_specs=[
            pl.BlockSpec((tt, PEER_HEADS * 2 * D_KEY_HALF), lambda i: (i, 0)),
            pl.BlockSpec((None, PEER_HEADS * 2, N_KEYS, D_KEY_HALF), lambda i: (layer, 0, 0, 0)),
        ],
        out_specs=[pl.BlockSpec((PEER_HEADS, N_KEYS, tt), lambda i: (0, 0, i))] * 4,
        out_shape=[
            jax.ShapeDtypeStruct((PEER_HEADS, N_KEYS, M), F32),
            jax.ShapeDtypeStruct((PEER_HEADS, N_KEYS, M), F32),
            jax.ShapeDtypeStruct((PEER_HEADS, N_KEYS, M), BF16),
            jax.ShapeDtypeStruct((PEER_HEADS, N_KEYS, M), BF16),
        ],
        scratch_shapes=[pltpu.VMEM((2, PEER_TOPK, tt), F32)],
        compiler_params=_cparams(("arbitrary",)),
        name="peer_topk",
    )(q, keys)


PEER_EC = 512


BF16_SUBLANES = 16


N_CHUNKS = N_EXPERTS // PEER_EC
GATE_TOKENS = 256


def _peer_dense_kernel(hT_ref, n_ref, a_ref, r1_ref, e1_ref, u_ref, v_ref, o_ref,
                       act_ref, w_ref, ub_ref, vb_ref):
    e = pl.program_id(1)
    tt = hT_ref.shape[1]

    @pl.when(e == 0)
    def _():
        o_ref[...] = jnp.zeros(o_ref.shape, F32)

    ub_ref[...] = u_ref[...].astype(BF16)
    vb_ref[...] = v_ref[...].astype(BF16)
    act_ref[...] = _dot(ub_ref[...], hT_ref[...])
    gate_tokens = min(GATE_TOKENS, tt)
    for tp in range(tt // gate_tokens):
        ts = slice(tp * gate_tokens, (tp + 1) * gate_tokens)
        for ii in range(PEER_EC // N_KEYS):
            i = e * (PEER_EC // N_KEYS) + ii
            g = None
            for h in range(PEER_HEADS):
                nrow = jnp.broadcast_to(n_ref[h, pl.ds(i, 1), ts], (BF16_SUBLANES, gate_tokens)).astype(BF16)
                arow = jnp.broadcast_to(a_ref[h, pl.ds(i, 1), ts], (BF16_SUBLANES, gate_tokens)).astype(BF16)
                gate = e1_ref[h, :, :, ts] * arow[None]
                val = jnp.where(r1_ref[h, :, :, ts] < nrow[None], gate, jnp.zeros_like(gate))
                g = val if g is None else g + val
            es = slice(ii * N_KEYS, (ii + 1) * N_KEYS)
            act = act_ref[es, ts]
            ge = (0.5 * act * (1.0 + lax.erf(act * (1.0 / math.sqrt(2.0))))).astype(BF16)
            w_ref[ts, es] = (ge * g.reshape(N_KEYS, gate_tokens)).T
    o_ref[...] += _dot(w_ref[...], vb_ref[...])


def peer_dense(hT, n, a, r1, e1, u_tab, v_tab, layer):
    D, M = hT.shape
    tt = min(M, 1024)
    groups = N_KEYS // BF16_SUBLANES
    r1 = r1.reshape(PEER_HEADS, groups, BF16_SUBLANES, M)
    e1 = e1.reshape(PEER_HEADS, groups, BF16_SUBLANES, M)
    once = pl.Buffered(1)
    return pl.pallas_call(
        _peer_dense_kernel,
        grid=(M // tt, N_CHUNKS),
        in_specs=[
            pl.BlockSpec((D, tt), lambda t, e: (0, t), pipeline_mode=once),
            pl.BlockSpec((PEER_HEADS, N_KEYS, tt), lambda t, e: (0, 0, t), pipeline_mode=once),
            pl.BlockSpec((PEER_HEADS, N_KEYS, tt), lambda t, e: (0, 0, t), pipeline_mode=once),
            pl.BlockSpec((PEER_HEADS, groups, BF16_SUBLANES, tt), lambda t, e: (0, 0, 0, t), pipeline_mode=once),
            pl.BlockSpec((PEER_HEADS, groups, BF16_SUBLANES, tt), lambda t, e: (0, 0, 0, t), pipeline_mode=once),
            pl.BlockSpec((None, PEER_EC, D), lambda t, e: (layer, e, 0)),
            pl.BlockSpec((None, PEER_EC, D), lambda t, e: (layer, e, 0)),
        ],
        out_specs=pl.BlockSpec((tt, D), lambda t, e: (t, 0), pipeline_mode=once),
        out_shape=jax.ShapeDtypeStruct((M, D), F32),
        scratch_shapes=[pltpu.VMEM((PEER_EC, tt), F32), pltpu.VMEM((tt, PEER_EC), BF16),
                        pltpu.VMEM((PEER_EC, D), BF16), pltpu.VMEM((PEER_EC, D), BF16)],
        compiler_params=_cparams(("arbitrary", "arbitrary")),
        name="peer_dense",
    )(hT, n, a, r1, e1, u_tab, v_tab)


def _t5_bucket(rel):
    n = jnp.maximum(rel, 0)
    max_exact = NUM_BUCKETS // 2
    nf = jnp.maximum(n, 1).astype(F32)
    large = max_exact + (jnp.log(nf / max_exact) / math.log(MAX_DISTANCE / max_exact)
                         * (NUM_BUCKETS - max_exact)).astype(jnp.int32)
    large = jnp.minimum(large, NUM_BUCKETS - 1)
    return jnp.where(n < max_exact, n, large)


def _bias_tables(rel_bias):
    rb = rel_bias.astype(F32)
    rel = jnp.arange(WINDOW, dtype=jnp.int32)[:, None] - (jnp.arange(2 * WINDOW, dtype=jnp.int32) - WINDOW)[None, :]
    valid = (rel >= 0) & (rel < WINDOW)
    tab = jnp.where(valid[:, :, None], rb[_t5_bucket(rel)], NEG)
    bias_p = (tab.transpose(2, 0, 1).reshape(N_KV_HEADS, PAIRS, 2, WINDOW, 2 * WINDOW)
              .transpose(0, 2, 1, 3, 4).reshape(N_KV_HEADS, 2, PAIRS * WINDOW, 2 * WINDOW))
    rel_s = WINDOW - jnp.arange(WINDOW, dtype=jnp.int32)
    tab_s = jnp.where((rel_s < WINDOW)[:, None], rb[_t5_bucket(rel_s)], NEG)
    bias_s = tab_s.T
    bias0 = rb[_t5_bucket(jnp.zeros((1,), jnp.int32))].T
    return bias_p, bias_s, bias0


def _peer_layer(x, g_ffn_i, sc, sh, gt, w_pq, sub_keys, u_tab, v_tab, i):
    G, R, D = x.shape
    h, hT = norm_mod(x, g_ffn_i, sc, sh, transposed=True)
    q = matmul(h.reshape(G * R, D), w_pq, i)
    n, a, r1, e1 = peer_topk(q, sub_keys, i)
    y = peer_dense(hT, n, a, r1, e1, u_tab, v_tab, i)
    return residual(x, y.reshape(G, R, D), gt)


def _split_mod(mod):
    return [mod[:, :, k * D_MODEL:(k + 1) * D_MODEL] for k in range(6)]


def kernel(x_prompt, x_sample, cache_k, cache_v, state_conv, c_prompt, c_sample, rel_bias,
           w_ada, b_ada, g_mix, g_ffn, g_final, w_qkv, b_qkv, w_o, sinks,
           w_pw1, b_pw1, w_dw, b_dw, ln_g, ln_b, w_pw2, b_pw2, w_pq, sub_keys, u_tab, v_tab):
    B, T, D = x_prompt.shape
    S = x_sample.shape[0]
    n_attn = w_qkv.shape[0]

    rows = B + S
    rows_pad = -(-rows // 16) * 16
    c_all = jnp.concatenate([c_prompt, c_sample, jnp.zeros((rows_pad - rows, D), F32)], axis=0)
    mod = ada_mod(c_all, w_ada, b_ada)
    bias_p, bias_s, bias0 = _bias_tables(rel_bias)

    xp = x_prompt
    xs = jnp.pad(x_sample.reshape(1, S, D), ((0, 0), (0, SAMPLE_PAD - S), (0, 0)))
    new_kp, new_vp, new_cp, new_ks, new_vs, new_cs = [], [], [], [], [], []
    tm_rows = T

    for i in range(DEPTH):
        mp = _split_mod(mod[i, :B].reshape(B, 1, 6 * D))
        ms = _split_mod(jnp.pad(mod[i, B:B + S], ((0, SAMPLE_PAD - S), (0, 0))).reshape(1, SAMPLE_PAD, 6 * D))
        hp = norm_mod(xp, g_mix[i], mp[1], mp[0]).reshape(B * T, D)
        hs = norm_mod(xs, g_mix[i], ms[1], ms[0]).reshape(SAMPLE_PAD, D)
        if i % 2 == 0:
            a = i // 2
            sink_a = sinks[a].astype(F32)
            qkv = matmul(hp, w_qkv, a, b_qkv)
            o = attn_prompt(qkv, bias_p, sink_a, T)
            xp = matmul(o, w_o, a, mode="resid", xres=xp.reshape(B * T, D), gate=mp[2],
                        rows_per_gate=tm_rows).reshape(B, T, D)
            kv = qkv.reshape(B, T, Q_DIM + 2 * KV_DIM)[:, T - WINDOW:, Q_DIM:]
            new_kp.append(kv[..., :KV_DIM].reshape(B, WINDOW, N_KV_HEADS, HEAD_DIM))
            new_vp.append(kv[..., KV_DIM:].reshape(B, WINDOW, N_KV_HEADS, HEAD_DIM))
            qkv_s = matmul(hs, w_qkv, a, b_qkv)
            q3 = qkv_s[:S, :Q_DIM].reshape(S, N_HEADS, HEAD_DIM)
            k_new = qkv_s[:S, Q_DIM:Q_DIM + KV_DIM]
            v_new = qkv_s[:S, Q_DIM + KV_DIM:]
            ck = cache_k[a].reshape(S, WINDOW, KV_DIM)
            cv = cache_v[a].reshape(S, WINDOW, KV_DIM)
            o_s = attn_sample(q3, k_new.reshape(S, 1, KV_DIM), v_new.reshape(S, 1, KV_DIM), ck, cv,
                              bias_s, bias0, sink_a.reshape(N_HEADS, 1))
            o_s = jnp.pad(o_s.reshape(S, Q_DIM), ((0, SAMPLE_PAD - S), (0, 0))).astype(BF16)
            xs = matmul(o_s, w_o, a, mode="resid",
                        xres=xs.reshape(SAMPLE_PAD, D), gate=ms[2]).reshape(1, SAMPLE_PAD, D)
            new_ks.append(jnp.concatenate([ck[:, 1:], k_new[:, None]], axis=1)
                          .reshape(S, WINDOW, N_KV_HEADS, HEAD_DIM))
            new_vs.append(jnp.concatenate([cv[:, 1:], v_new[:, None]], axis=1)
                          .reshape(S, WINDOW, N_KV_HEADS, HEAD_DIM))
        else:
            bl = i // 2
            u = matmul(hp, w_pw1, bl, b_pw1, mode="glu", n_out=D).reshape(B, T, D)
            y = conv_prompt(u, w_dw, b_dw, bl)
            z = ln_swish(y.reshape(B * T, D), ln_g, ln_b, bl)
            xp = matmul(z, w_pw2, bl, b_pw2, mode="resid", xres=xp.reshape(B * T, D), gate=mp[2],
                        rows_per_gate=tm_rows).reshape(B, T, D)
            new_cp.append(u[:, T - (CONV_WIDTH - 1):])
            u_s = matmul(hs, w_pw1, bl, b_pw1, mode="glu", n_out=D)
            st = state_conv[bl]
            y_s = conv_sample(st.transpose(1, 0, 2), u_s[:S], w_dw, b_dw, bl)
            z_s = ln_swish(jnp.pad(y_s, ((0, SAMPLE_PAD - S), (0, 0))), ln_g, ln_b, bl)
            xs = matmul(z_s, w_pw2, bl, b_pw2, mode="resid", xres=xs.reshape(SAMPLE_PAD, D),
                        gate=ms[2]).reshape(1, SAMPLE_PAD, D)
            new_cs.append(jnp.concatenate([st[:, 1:], u_s[:S, None]], axis=1))
        xp = _peer_layer(xp, g_ffn[i], mp[4], mp[3], mp[5], w_pq, sub_keys, u_tab, v_tab, i)
        xs = _peer_layer(xs, g_ffn[i], ms[4], ms[3], ms[5], w_pq, sub_keys, u_tab, v_tab, i)

    y_prompt = norm_mod(xp, g_final, out_dtype=F32)
    y_sample = norm_mod(xs, g_final, out_dtype=F32)[0, :S].reshape(S, 1, D)
    return (y_prompt, y_sample, jnp.stack(new_kp), jnp.stack(new_vp), jnp.stack(new_cp),
            jnp.stack(new_ks), jnp.stack(new_vs), jnp.stack(new_cs))
```

```python
import functools
import math

import jax
import jax.numpy as jnp
import numpy as np
from jax import lax
from jax.experimental import pallas as pl
from jax.experimental.pallas import tpu as pltpu

D_MODEL = 2048
DEPTH = 4
N_HEADS = 64
N_KV_HEADS = 8
HEAD_DIM = 64
GROUP = N_HEADS // N_KV_HEADS
Q_DIM = N_HEADS * HEAD_DIM
KV_DIM = N_KV_HEADS * HEAD_DIM
WINDOW = 128
NUM_BUCKETS = 32
MAX_DISTANCE = 128
CONV_WIDTH = 31
PEER_HEADS = 8
N_KEYS = 128
N_EXPERTS = N_KEYS * N_KEYS
PEER_TOPK = 16
D_KEY_HALF = 128
EPS = 1e-6
NEG = -1e30

VMEM_LIMIT_V7X = 56 * 1024 * 1024
SAMPLE_PAD = 128

BF16 = jnp.bfloat16
F32 = jnp.float32


def _cparams(sem, flags=None):
    return pltpu.CompilerParams(dimension_semantics=sem, vmem_limit_bytes=VMEM_LIMIT_V7X, flags=flags)


def _dot(a, b):
    return jnp.dot(a, b, preferred_element_type=F32)


def _dot_nt(a, b):
    return lax.dot_general(a, b, (((1,), (1,)), ((), ())), preferred_element_type=F32)


def _dot_tn(a, b):
    return lax.dot_general(a, b, (((0,), (0,)), ((), ())), preferred_element_type=F32)


def _ada_kernel(c_ref, w_ref, b_ref, o_ref):
    c = c_ref[...]
    cond = (c * jax.nn.sigmoid(c)).astype(BF16)
    o_ref[...] = _dot(cond, w_ref[...].astype(BF16)) + b_ref[...]


def ada_mod(c_all, w_ada, b_ada):
    rows = c_all.shape[0]
    n = w_ada.shape[2]
    tn = 1536
    return pl.pallas_call(
        _ada_kernel,
        grid=(DEPTH, n // tn),
        in_specs=[
            pl.BlockSpec((rows, D_MODEL), lambda l, j: (0, 0)),
            pl.BlockSpec((None, D_MODEL, tn), lambda l, j: (l, 0, j)),
            pl.BlockSpec((None, 1, tn), lambda l, j: (l, 0, j)),
        ],
        out_specs=pl.BlockSpec((None, rows, tn), lambda l, j: (l, 0, j)),
        out_shape=jax.ShapeDtypeStruct((DEPTH, rows, n), F32),
        compiler_params=_cparams(("arbitrary", "arbitrary")),
        name="ada_mod",
    )(c_all, w_ada, b_ada.reshape(DEPTH, 1, n))


def _norm_kernel(*refs, modulated, transposed):
    if modulated:
        x_ref, g_ref, sc_ref, sh_ref = refs[:4]
        outs = refs[4:]
    else:
        x_ref, g_ref = refs[:2]
        outs = refs[2:]
    x = x_ref[0]
    y = x * lax.rsqrt(jnp.mean(x * x, axis=-1, keepdims=True) + EPS) * g_ref[...]
    if modulated:
        y = y * (1.0 + sc_ref[0]) + sh_ref[0]
    outs[0][0] = y.astype(outs[0].dtype)
    if transposed:
        outs[1][...] = y.T.astype(outs[1].dtype)


def norm_mod(x, g, sc=None, sh=None, *, out_dtype=BF16, transposed=False):
    G, R, D = x.shape
    tt = min(R, 512)
    nt = R // tt
    modulated = sc is not None
    in_specs = [
        pl.BlockSpec((1, tt, D), lambda b, i: (b, i, 0)),
        pl.BlockSpec((1, D), lambda b, i: (0, 0)),
    ]
    args = [x, g.reshape(1, D)]
    if modulated:
        rm = sc.shape[1]
        if rm == 1:
            mspec = pl.BlockSpec((1, 1, D), lambda b, i: (b, 0, 0))
        else:
            mspec = pl.BlockSpec((1, tt, D), lambda b, i: (b, i, 0))
        in_specs += [mspec, mspec]
        args += [sc, sh]
    out_specs = [pl.BlockSpec((1, tt, D), lambda b, i: (b, i, 0))]
    out_shape = [jax.ShapeDtypeStruct((G, R, D), out_dtype)]
    if transposed:
        out_specs.append(pl.BlockSpec((D, tt), lambda b, i: (0, b * nt + i)))
        out_shape.append(jax.ShapeDtypeStruct((D, G * R), BF16))
    res = pl.pallas_call(
        functools.partial(_norm_kernel, modulated=modulated, transposed=transposed),
        grid=(G, nt),
        in_specs=in_specs,
        out_specs=out_specs,
        out_shape=out_shape,
        compiler_params=_cparams(("arbitrary", "arbitrary")),
        name="norm_mod",
    )(*args)
    return res if transposed else res[0]


def _mm_kernel(*refs, mode, has_bias):
    it = iter(refs)
    h_ref = next(it)
    w_ref = next(it)
    w2_ref = next(it) if mode == "glu" else None
    b_ref = next(it) if has_bias else None
    b2_ref = next(it) if (mode == "glu" and has_bias) else None
    if mode == "resid":
        x_ref = next(it)
        gate_ref = next(it)
    o_ref = next(it)
    ws_ref = next(it)
    ws2_ref = next(it) if mode == "glu" else None

    @pl.when(pl.program_id(1) == 0)
    def _():
        ws_ref[...] = w_ref[...].astype(BF16)
        if mode == "glu":
            ws2_ref[...] = w2_ref[...].astype(BF16)

    h = h_ref[...]
    acc = _dot(h, ws_ref[...])
    if has_bias:
        acc = acc + b_ref[...]
    if mode == "glu":
        gte = _dot(h, ws2_ref[...])
        if has_bias:
            gte = gte + b2_ref[...]
        acc = acc * jax.nn.sigmoid(gte)
    if mode == "resid":
        acc = x_ref[...] + gate_ref[0] * acc
    o_ref[...] = acc.astype(o_ref.dtype)


def matmul(h, w, layer, bias=None, *, mode="plain", n_out=None, xres=None, gate=None,
           rows_per_gate=None, out_dtype=F32, tn=512):
    M, K = h.shape
    nw = w.shape[2]
    N = n_out if n_out is not None else nw
    tm = min(M, 1024)
    nj, ni = N // tn, M // tm
    has_bias = bias is not None
    in_specs = [
        pl.BlockSpec((tm, K), lambda j, i: (i, 0)),
        pl.BlockSpec((None, K, tn), lambda j, i: (layer, 0, j)),
    ]
    args = [h, w]
    if mode == "glu":
        in_specs.append(pl.BlockSpec((None, K, tn), lambda j, i: (layer, 0, j + nj)))
        args.append(w)
    if has_bias:
        b3 = bias.reshape(bias.shape[0], 1, nw)
        in_specs.append(pl.BlockSpec((None, 1, tn), lambda j, i: (layer, 0, j)))
        args.append(b3)
        if mode == "glu":
            in_specs.append(pl.BlockSpec((None, 1, tn), lambda j, i: (layer, 0, j + nj)))
            args.append(b3)
    if mode == "resid":
        in_specs.append(pl.BlockSpec((tm, tn), lambda j, i: (i, j)))
        args.append(xres)
        if gate.shape[1] == 1:
            tiles_per_gate = rows_per_gate // tm
            in_specs.append(pl.BlockSpec((1, 1, tn), lambda j, i: (i // tiles_per_gate, 0, j)))
        else:
            in_specs.append(pl.BlockSpec((1, tm, tn), lambda j, i: (0, i, j)))
        args.append(gate)
    scratch = [pltpu.VMEM((K, tn), BF16)]
    if mode == "glu":
        scratch.append(pltpu.VMEM((K, tn), BF16))
    return pl.pallas_call(
        functools.partial(_mm_kernel, mode=mode, has_bias=has_bias),
        grid=(nj, ni),
        in_specs=in_specs,
        out_specs=pl.BlockSpec((tm, tn), lambda j, i: (i, j)),
        out_shape=jax.ShapeDtypeStruct((M, N), out_dtype),
        scratch_shapes=scratch,
        compiler_params=_cparams(("arbitrary", "arbitrary")),
        name="proj_" + mode,
    )(*args)


def _resid_kernel(x_ref, y_ref, g_ref, o_ref):
    o_ref[0] = x_ref[0] + g_ref[0] * y_ref[0]


def residual(x, y, gate):
    G, R, D = x.shape
    tt = min(R, 512)
    if gate.shape[1] == 1:
        gspec = pl.BlockSpec((1, 1, D), lambda b, i: (b, 0, 0))
    else:
        gspec = pl.BlockSpec((1, tt, D), lambda b, i: (b, i, 0))
    spec = pl.BlockSpec((1, tt, D), lambda b, i: (b, i, 0))
    return pl.pallas_call(
        _resid_kernel,
        grid=(G, R // tt),
        in_specs=[spec, spec, gspec],
        out_specs=spec,
        out_shape=jax.ShapeDtypeStruct((G, R, D), F32),
        compiler_params=_cparams(("arbitrary", "arbitrary")),
        name="residual",
    )(x, y, gate)


PAIRS = GROUP // 2


def _attn_prompt_kernel(q_ref, kc_ref, kp_ref, vc_ref, vp_ref, bias_ref, sink_ref, o_ref,
                        *, blocks_per_seq):
    first = (pl.program_id(0) % blocks_per_seq) == 0
    col = lax.broadcasted_iota(jnp.int32, (1, 2 * WINDOW), 1)
    pen = jnp.where(jnp.logical_and(first, col < WINDOW), NEG, 0.0).astype(F32)
    zeros = jnp.zeros((2 * WINDOW, HEAD_DIM), BF16)
    rows = PAIRS * WINDOW
    for g in range(N_KV_HEADS):
        ks = slice(g * HEAD_DIM, (g + 1) * HEAD_DIM)
        kcat = jnp.concatenate([kp_ref[:, ks], kc_ref[:, ks]], axis=0).astype(BF16)
        vcat = jnp.concatenate([vp_ref[:, ks], vc_ref[:, ks]], axis=0).astype(BF16)
        qs = slice(g * GROUP * HEAD_DIM, (g + 1) * GROUP * HEAD_DIM)
        qg = jnp.concatenate(
            [q_ref[:, qs.start + pp * 2 * HEAD_DIM:qs.start + (pp + 1) * 2 * HEAD_DIM] for pp in range(PAIRS)],
            axis=0)
        qg = (qg * (HEAD_DIM ** -0.5)).astype(BF16)
        out = None
        for par in range(2):
            kx = jnp.concatenate([kcat, zeros] if par == 0 else [zeros, kcat], axis=1)
            vx = jnp.concatenate([vcat, zeros] if par == 0 else [zeros, vcat], axis=1)
            s = _dot_nt(qg, kx) + (bias_ref[g, par] + pen)
            sink = jnp.concatenate(
                [jnp.full((WINDOW, 1), sink_ref[g * GROUP + 2 * pp + par], F32) for pp in range(PAIRS)], axis=0)
            m = jnp.maximum(jnp.max(s, axis=-1, keepdims=True), sink)
            p = jnp.exp(s - m)
            den = jnp.sum(p, axis=-1, keepdims=True) + jnp.exp(sink - m)
            o = _dot(p.astype(BF16), vx) / den
            out = o if out is None else out + o
        for pp in range(PAIRS):
            o_ref[:, qs.start + pp * 2 * HEAD_DIM:qs.start + (pp + 1) * 2 * HEAD_DIM] = (
                out[pp * WINDOW:(pp + 1) * WINDOW].astype(o_ref.dtype))


def attn_prompt(qkv, bias_tab, sinks, seq_len):
    M = qkv.shape[0]
    nb = M // WINDOW
    bps = seq_len // WINDOW
    kcol = Q_DIM // KV_DIM
    prev = lambda r: jnp.maximum(r - 1, 0)
    return pl.pallas_call(
        functools.partial(_attn_prompt_kernel, blocks_per_seq=bps),
        grid=(nb,),
        in_specs=[
            pl.BlockSpec((WINDOW, Q_DIM), lambda r: (r, 0)),
            pl.BlockSpec((WINDOW, KV_DIM), lambda r: (r, kcol)),
            pl.BlockSpec((WINDOW, KV_DIM), lambda r: (prev(r), kcol)),
            pl.BlockSpec((WINDOW, KV_DIM), lambda r: (r, kcol + 1)),
            pl.BlockSpec((WINDOW, KV_DIM), lambda r: (prev(r), kcol + 1)),
            pl.BlockSpec((N_KV_HEADS, 2, PAIRS * WINDOW, 2 * WINDOW), lambda r: (0, 0, 0, 0),
                         pipeline_mode=pl.Buffered(1)),
            pl.BlockSpec(memory_space=pltpu.SMEM),
        ],
        out_specs=pl.BlockSpec((WINDOW, Q_DIM), lambda r: (r, 0)),
        out_shape=jax.ShapeDtypeStruct((M, Q_DIM), BF16),
        compiler_params=_cparams(("arbitrary",)),
        name="attn_prompt",
    )(qkv, qkv, qkv, qkv, qkv, bias_tab, sinks)


def _attn_sample_kernel(q_ref, kn_ref, vn_ref, ck_ref, cv_ref, bias_ref, bias0_ref, sink_ref, o_ref):
    for g in range(N_KV_HEADS):
        ks = slice(g * HEAD_DIM, (g + 1) * HEAD_DIM)
        hs = slice(g * GROUP, (g + 1) * GROUP)
        qg = q_ref[0, hs, :] * (HEAD_DIM ** -0.5)
        kn = kn_ref[0, :, ks]
        vn = vn_ref[0, :, ks]
        s = _dot_nt(qg.astype(BF16), ck_ref[0, :, ks].astype(BF16)) + bias_ref[hs, :]
        s_new = jnp.sum(qg * kn, axis=-1, keepdims=True) + bias0_ref[hs, :]
        sink = sink_ref[hs, :]
        m = jnp.maximum(jnp.maximum(jnp.max(s, axis=-1, keepdims=True), s_new), sink)
        p = jnp.exp(s - m)
        pn = jnp.exp(s_new - m)
        den = jnp.sum(p, axis=-1, keepdims=True) + pn + jnp.exp(sink - m)
        o = (_dot(p.astype(BF16), cv_ref[0, :, ks].astype(BF16)) + pn * vn) / den
        o_ref[0, hs, :] = o.astype(o_ref.dtype)


def attn_sample(q3, k_new, v_new, cache_k, cache_v, bias_s, bias0, sinks):
    nseq = cache_k.shape[0]
    return pl.pallas_call(
        _attn_sample_kernel,
        grid=(nseq,),
        in_specs=[
            pl.BlockSpec((1, N_HEADS, HEAD_DIM), lambda b: (b, 0, 0)),
            pl.BlockSpec((1, 1, KV_DIM), lambda b: (b, 0, 0)),
            pl.BlockSpec((1, 1, KV_DIM), lambda b: (b, 0, 0)),
            pl.BlockSpec((1, WINDOW, KV_DIM), lambda b: (b, 0, 0)),
            pl.BlockSpec((1, WINDOW, KV_DIM), lambda b: (b, 0, 0)),
            pl.BlockSpec((N_HEADS, WINDOW), lambda b: (0, 0)),
            pl.BlockSpec((N_HEADS, 1), lambda b: (0, 0)),
            pl.BlockSpec((N_HEADS, 1), lambda b: (0, 0)),
        ],
        out_specs=pl.BlockSpec((1, N_HEADS, HEAD_DIM), lambda b: (b, 0, 0)),
        out_shape=jax.ShapeDtypeStruct((nseq, N_HEADS, HEAD_DIM), F32),
        compiler_params=_cparams(("arbitrary",)),
        name="attn_sample",
    )(q3, k_new, v_new, cache_k, cache_v, bias_s, bias0, sinks)


CONV_HALO = 32
F32_SUBLANES = 8


def _conv_prompt_kernel(u_ref, halo_ref, w_ref, b_ref, o_ref, scr_ref, *, tt):
    first = pl.program_id(1) == 0

    @pl.when(first)
    def _():
        scr_ref[0:CONV_HALO, :] = jnp.zeros((CONV_HALO, scr_ref.shape[1]), F32)

    @pl.when(jnp.logical_not(first))
    def _():
        scr_ref[0:CONV_HALO, :] = halo_ref[0]

    scr_ref[CONV_HALO:, :] = u_ref[0]
    off = CONV_HALO - (CONV_WIDTH - 1)
    acc = jnp.broadcast_to(b_ref[...], (tt, scr_ref.shape[1]))
    for s in range(F32_SUBLANES):
        rows = tt if s == 0 else tt + F32_SUBLANES
        part = None
        for q in range(off, off + CONV_WIDTH):
            if q % F32_SUBLANES != s:
                continue
            term = scr_ref[q - s:q - s + rows, :] * w_ref[q - off:q - off + 1, :]
            part = term if part is None else part + term
        acc = acc + (part if s == 0 else part[s:s + tt])
    o_ref[0] = acc


def conv_prompt(u, w_dw, b_dw, layer):
    B, T, D = u.shape
    tt, dc = 512, 512
    hb = tt // CONV_HALO
    return pl.pallas_call(
        functools.partial(_conv_prompt_kernel, tt=tt),
        grid=(B, T // tt, D // dc),
        in_specs=[
            pl.BlockSpec((1, tt, dc), lambda b, i, j: (b, i, j)),
            pl.BlockSpec((1, CONV_HALO, dc), lambda b, i, j: (b, jnp.maximum(i * hb - 1, 0), j)),
            pl.BlockSpec((None, CONV_WIDTH, dc), lambda b, i, j: (layer, 0, j)),
            pl.BlockSpec((None, 1, dc), lambda b, i, j: (layer, 0, j)),
        ],
        out_specs=pl.BlockSpec((1, tt, dc), lambda b, i, j: (b, i, j)),
        out_shape=jax.ShapeDtypeStruct((B, T, D), F32),
        scratch_shapes=[pltpu.VMEM((CONV_HALO + tt, dc), F32)],
        compiler_params=_cparams(("arbitrary", "arbitrary", "arbitrary")),
        name="conv_prompt",
    )(u, u, w_dw, b_dw.reshape(b_dw.shape[0], 1, D))


def _conv_sample_kernel(st_ref, u_ref, w_ref, b_ref, o_ref):
    acc = u_ref[...] * w_ref[CONV_WIDTH - 1:CONV_WIDTH, :] + b_ref[...]
    for w in range(CONV_WIDTH - 1):
        acc = acc + st_ref[w] * w_ref[w:w + 1, :]
    o_ref[...] = acc


def conv_sample(state_t, u, w_dw, b_dw, layer):
    nseq, D = u.shape
    return pl.pallas_call(
        _conv_sample_kernel,
        grid=(1,),
        in_specs=[
            pl.BlockSpec((CONV_WIDTH - 1, nseq, D), lambda i: (0, 0, 0)),
            pl.BlockSpec((nseq, D), lambda i: (0, 0)),
            pl.BlockSpec((None, CONV_WIDTH, D), lambda i: (layer, 0, 0)),
            pl.BlockSpec((None, 1, D), lambda i: (layer, 0, 0)),
        ],
        out_specs=pl.BlockSpec((nseq, D), lambda i: (0, 0)),
        out_shape=jax.ShapeDtypeStruct((nseq, D), F32),
        compiler_params=_cparams(("arbitrary",)),
        name="conv_sample",
    )(state_t, u, w_dw, b_dw.reshape(b_dw.shape[0], 1, D))


def _ln_swish_kernel(y_ref, g_ref, b_ref, o_ref):
    y = y_ref[...]
    mu = jnp.mean(y, axis=-1, keepdims=True)
    yc = y - mu
    var = jnp.mean(yc * yc, axis=-1, keepdims=True)
    z = yc * lax.rsqrt(var + EPS) * g_ref[...] + b_ref[...]
    o_ref[...] = (z * jax.nn.sigmoid(z)).astype(o_ref.dtype)


def ln_swish(y, ln_g, ln_b, layer):
    M, D = y.shape
    tt = min(M, 512)
    return pl.pallas_call(
        _ln_swish_kernel,
        grid=(M // tt,),
        in_specs=[
            pl.BlockSpec((tt, D), lambda i: (i, 0)),
            pl.BlockSpec((None, 1, D), lambda i: (layer, 0, 0)),
            pl.BlockSpec((None, 1, D), lambda i: (layer, 0, 0)),
        ],
        out_specs=pl.BlockSpec((tt, D), lambda i: (i, 0)),
        out_shape=jax.ShapeDtypeStruct((M, D), BF16),
        compiler_params=_cparams(("arbitrary",)),
        name="ln_swish",
    )(y, ln_g.reshape(-1, 1, D), ln_b.reshape(-1, 1, D))


NOT_RANKED = 127.0


def _top_values(s, k, with_rank=False):
    out = []
    rank = jnp.full(s.shape, NOT_RANKED, F32) if with_rank else None
    for i in range(k):
        m = jnp.max(s, axis=0, keepdims=True)
        out.append(m)
        hit = s == m
        if with_rank:
            rank = jnp.where(hit, float(i), rank)
        s = jnp.where(hit, -jnp.inf, s)
    return out, rank


def _peer_topk_kernel(q_ref, keys_ref, n_ref, a_ref, r1_ref, e1_ref, top_ref):
    for h in range(PEER_HEADS):
        halves = []
        for p in range(2):
            hp = 2 * h + p
            qs = q_ref[:, hp * D_KEY_HALF:(hp + 1) * D_KEY_HALF].astype(BF16)
            s = _dot_nt(keys_ref[hp].astype(BF16), qs)
            tops, rank = _top_values(s, PEER_TOPK, with_rank=(p == 1))
            for k, m in enumerate(tops):
                top_ref[p, k:k + 1, :] = m
            halves.append((s, tops, rank))
        (s0, t0, _), (s1, t1, rank1) = halves
        t1_all = top_ref[1]
        pieces = ([t0[0] + t1_all] + [t0[a] + t1_all[0:8] for a in range(1, 8)]
                  + [top_ref[0, 8:PEER_TOPK, :] + t1[0]])
        best, _ = _top_values(jnp.concatenate(pieces, axis=0), PEER_TOPK)
        tau = best[-1]
        z = jnp.exp(best[0] - best[0])
        for v in best[1:]:
            z = z + jnp.exp(v - best[0])
        n = jnp.zeros(s0.shape, F32)
        for a in range(PEER_TOPK):
            cnt = jnp.sum(jnp.where(t0[a] + t1_all >= tau, 1.0, 0.0), axis=0, keepdims=True)
            n = jnp.where(s0 == t0[a], cnt, n)
        n_ref[h] = n
        a_ref[h] = jnp.exp(s0 - t0[0])
        r1_ref[h] = rank1.astype(BF16)
        e1_ref[h] = (jnp.exp(s1 - t1[0]) / z).astype(BF16)


def peer_topk(q, sub_keys, layer):
    M = q.shape[0]
    tt = min(M, 256)
    keys = sub_keys.reshape(DEPTH, PEER_HEADS * 2, N_KEYS, D_KEY_HALF)
    return pl.pallas_call(
        _peer_topk_kernel,
        grid=(M // tt,),
        in_specs=[
            pl.BlockSpec((tt, PEER_HEADS * 2 * D_KEY_HALF), lambda i: (i, 0)),
            pl.BlockSpec((None, PEER_HEADS * 2, N_KEYS, D_KEY_HALF), lambda i: (layer, 0, 0, 0)),
        ],
        out_specs=[pl.BlockSpec((PEER_HEADS, N_KEYS, tt), lambda i: (0, 0, i))] * 4,
        out_shape=[
            jax.ShapeDtypeStruct((PEER_HEADS, N_KEYS, M), F32),
            jax.ShapeDtypeStruct((PEER_HEADS, N_KEYS, M), F32),
            jax.ShapeDtypeStruct((PEER_HEADS, N_KEYS, M), BF16),
            jax.ShapeDtypeStruct((PEER_HEADS, N_KEYS, M), BF16),
        ],
        scratch_shapes=[pltpu.VMEM((2, PEER_TOPK, tt), F32)],
        compiler_params=_cparams(("arbitrary",)),
        name="peer_topk",
    )(q, keys)


PEER_EC = 512


BF16_SUBLANES = 16


N_CHUNKS = N_EXPERTS // PEER_EC
GATE_TOKENS = 256


def _bcast_row_bf16(ref, h, i, ts):
    row = ref[h, pl.ds(i, 1), ts]
    rep = jnp.broadcast_to(row, (F32_SUBLANES, row.shape[1]))
    return jnp.concatenate([rep, rep], axis=0).astype(BF16)


def _peer_dense_kernel(hT_ref, n_ref, a_ref, r1_ref, e1_ref, u_ref, v_ref, o_ref,
                       act_ref, w_ref, ub_ref, vb_ref):
    e = pl.program_id(1)
    tt = hT_ref.shape[1]

    @pl.when(e == 0)
    def _():
        o_ref[...] = jnp.zeros(o_ref.shape, F32)

    ub_ref[...] = u_ref[...].astype(BF16)
    vb_ref[...] = v_ref[...].astype(BF16)
    act_ref[...] = _dot(ub_ref[...], hT_ref[...])
    gate_tokens = min(GATE_TOKENS, tt)
    for tp in range(tt // gate_tokens):
        ts = slice(tp * gate_tokens, (tp + 1) * gate_tokens)
        for ii in range(PEER_EC // N_KEYS):
            i = e * (PEER_EC // N_KEYS) + ii
            g = None
            for h in range(PEER_HEADS):
                nrow = _bcast_row_bf16(n_ref, h, i, ts)
                arow = _bcast_row_bf16(a_ref, h, i, ts)
                gate = e1_ref[h, :, :, ts] * arow[None]
                val = jnp.where(r1_ref[h, :, :, ts] < nrow[None], gate, jnp.zeros_like(gate))
                g = val if g is None else g + val
            es = slice(ii * N_KEYS, (ii + 1) * N_KEYS)
            act = act_ref[es, ts]
            ge = (0.5 * act * (1.0 + lax.erf(act * (1.0 / math.sqrt(2.0))))).astype(BF16)
            w_ref[ts, es] = (ge * g.reshape(N_KEYS, gate_tokens)).T
    o_ref[...] += _dot(w_ref[...], vb_ref[...])


def peer_dense(hT, n, a, r1, e1, u_tab, v_tab, layer):
    D, M = hT.shape
    tt = min(M, 1024)
    groups = N_KEYS // BF16_SUBLANES
    r1 = r1.reshape(PEER_HEADS, groups, BF16_SUBLANES, M)
    e1 = e1.reshape(PEER_HEADS, groups, BF16_SUBLANES, M)
    once = pl.Buffered(1)
    return pl.pallas_call(
        _peer_dense_kernel,
        grid=(M // tt, N_CHUNKS),
        in_specs=[
            pl.BlockSpec((D, tt), lambda t, e: (0, t), pipeline_mode=once),
            pl.BlockSpec((PEER_HEADS, N_KEYS, tt), lambda t, e: (0, 0, t), pipeline_mode=once),
            pl.BlockSpec((PEER_HEADS, N_KEYS, tt), lambda t, e: (0, 0, t), pipeline_mode=once),
            pl.BlockSpec((PEER_HEADS, groups, BF16_SUBLANES, tt), lambda t, e: (0, 0, 0, t), pipeline_mode=once),
            pl.BlockSpec((PEER_HEADS, groups, BF16_SUBLANES, tt), lambda t, e: (0, 0, 0, t), pipeline_mode=once),
            pl.BlockSpec((None, PEER_EC, D), lambda t, e: (layer, e, 0)),
            pl.BlockSpec((None, PEER_EC, D), lambda t, e: (layer, e, 0)),
        ],
        out_specs=pl.BlockSpec((tt, D), lambda t, e: (t, 0), pipeline_mode=once),
        out_shape=jax.ShapeDtypeStruct((M, D), F32),
        scratch_shapes=[pltpu.VMEM((PEER_EC, tt), F32), pltpu.VMEM((tt, PEER_EC), BF16),
                        pltpu.VMEM((PEER_EC, D), BF16), pltpu.VMEM((PEER_EC, D), BF16)],
        compiler_params=_cparams(("arbitrary", "arbitrary")),
        name="peer_dense",
    )(hT, n, a, r1, e1, u_tab, v_tab)


def _t5_bucket(rel):
    n = jnp.maximum(rel, 0)
    max_exact = NUM_BUCKETS // 2
    nf = jnp.maximum(n, 1).astype(F32)
    large = max_exact + (jnp.log(nf / max_exact) / math.log(MAX_DISTANCE / max_exact)
                         * (NUM_BUCKETS - max_exact)).astype(jnp.int32)
    large = jnp.minimum(large, NUM_BUCKETS - 1)
    return jnp.where(n < max_exact, n, large)


def _bias_tables(rel_bias):
    rb = rel_bias.astype(F32)
    per_rel = rb[_t5_bucket(jnp.arange(WINDOW, dtype=jnp.int32))].T
    span = 3 * WINDOW - 1
    line = jnp.full((N_HEADS, span), NEG, F32).at[:, WINDOW:2 * WINDOW].set(per_rel[:, ::-1])
    skew = jnp.tile(line, (1, WINDOW + 1))[:, :WINDOW * (span + 1)].reshape(N_HEADS, WINDOW, span + 1)
    tab = skew[:, ::-1, :2 * WINDOW]
    bias_p = (tab.reshape(N_KV_HEADS, PAIRS, 2, WINDOW, 2 * WINDOW)
              .transpose(0, 2, 1, 3, 4).reshape(N_KV_HEADS, 2, PAIRS * WINDOW, 2 * WINDOW))
    rel_s = WINDOW - jnp.arange(WINDOW, dtype=jnp.int32)
    tab_s = jnp.where((rel_s < WINDOW)[:, None], rb[_t5_bucket(rel_s)], NEG)
    bias_s = tab_s.T
    bias0 = rb[_t5_bucket(jnp.zeros((1,), jnp.int32))].T
    return bias_p, bias_s, bias0


def _peer_layer(x, g_ffn_i, sc, sh, gt, w_pq, sub_keys, u_tab, v_tab, i):
    G, R, D = x.shape
    h, hT = norm_mod(x, g_ffn_i, sc, sh, transposed=True)
    q = matmul(h.reshape(G * R, D), w_pq, i)
    n, a, r1, e1 = peer_topk(q, sub_keys, i)
    y = peer_dense(hT, n, a, r1, e1, u_tab, v_tab, i)
    return residual(x, y.reshape(G, R, D), gt)


def _split_mod(mod):
    return [mod[:, :, k * D_MODEL:(k + 1) * D_MODEL] for k in range(6)]


def kernel(x_prompt, x_sample, cache_k, cache_v, state_conv, c_prompt, c_sample, rel_bias,
           w_ada, b_ada, g_mix, g_ffn, g_final, w_qkv, b_qkv, w_o, sinks,
           w_pw1, b_pw1, w_dw, b_dw, ln_g, ln_b, w_pw2, b_pw2, w_pq, sub_keys, u_tab, v_tab):
    B, T, D = x_prompt.shape
    S = x_sample.shape[0]
    n_attn = w_qkv.shape[0]

    rows = B + S
    rows_pad = -(-rows // 16) * 16
    c_all = jnp.concatenate([c_prompt, c_sample, jnp.zeros((rows_pad - rows, D), F32)], axis=0)
    mod = ada_mod(c_all, w_ada, b_ada)
    bias_p, bias_s, bias0 = _bias_tables(rel_bias)

    xp = x_prompt
    xs = jnp.pad(x_sample.reshape(1, S, D), ((0, 0), (0, SAMPLE_PAD - S), (0, 0)))
    new_kp, new_vp, new_cp, new_ks, new_vs, new_cs = [], [], [], [], [], []
    tm_rows = T

    for i in range(DEPTH):
        mp = _split_mod(mod[i, :B].reshape(B, 1, 6 * D))
        ms = _split_mod(jnp.pad(mod[i, B:B + S], ((0, SAMPLE_PAD - S), (0, 0))).reshape(1, SAMPLE_PAD, 6 * D))
        hp = norm_mod(xp, g_mix[i], mp[1], mp[0]).reshape(B * T, D)
        hs = norm_mod(xs, g_mix[i], ms[1], ms[0]).reshape(SAMPLE_PAD, D)
        if i % 2 == 0:
            a = i // 2
            sink_a = sinks[a].astype(F32)
            qkv = matmul(hp, w_qkv, a, b_qkv)
            o = attn_prompt(qkv, bias_p, sink_a, T)
            xp = matmul(o, w_o, a, mode="resid", xres=xp.reshape(B * T, D), gate=mp[2],
                        rows_per_gate=tm_rows).reshape(B, T, D)
            kv = qkv.reshape(B, T, Q_DIM + 2 * KV_DIM)[:, T - WINDOW:, Q_DIM:]
            new_kp.append(kv[..., :KV_DIM].reshape(B, WINDOW, N_KV_HEADS, HEAD_DIM))
            new_vp.append(kv[..., KV_DIM:].reshape(B, WINDOW, N_KV_HEADS, HEAD_DIM))
            qkv_s = matmul(hs, w_qkv, a, b_qkv)
            q3 = qkv_s[:S, :Q_DIM].reshape(S, N_HEADS, HEAD_DIM)
            k_new = qkv_s[:S, Q_DIM:Q_DIM + KV_DIM]
            v_new = qkv_s[:S, Q_DIM + KV_DIM:]
            ck = cache_k[a].reshape(S, WINDOW, KV_DIM)
            cv = cache_v[a].reshape(S, WINDOW, KV_DIM)
            o_s = attn_sample(q3, k_new.reshape(S, 1, KV_DIM), v_new.reshape(S, 1, KV_DIM), ck, cv,
                              bias_s, bias0, sink_a.reshape(N_HEADS, 1))
            o_s = jnp.pad(o_s.reshape(S, Q_DIM), ((0, SAMPLE_PAD - S), (0, 0))).astype(BF16)
            xs = matmul(o_s, w_o, a, mode="resid",
                        xres=xs.reshape(SAMPLE_PAD, D), gate=ms[2]).reshape(1, SAMPLE_PAD, D)
            new_ks.append(jnp.concatenate([ck[:, 1:], k_new[:, None]], axis=1)
                          .reshape(S, WINDOW, N_KV_HEADS, HEAD_DIM))
            new_vs.append(jnp.concatenate([cv[:, 1:], v_new[:, None]], axis=1)
                          .reshape(S, WINDOW, N_KV_HEADS, HEAD_DIM))
        else:
            bl = i // 2
            u = matmul(hp, w_pw1, bl, b_pw1, mode="glu", n_out=D).reshape(B, T, D)
            y = conv_prompt(u, w_dw, b_dw, bl)
            z = ln_swish(y.reshape(B * T, D), ln_g, ln_b, bl)
            xp = matmul(z, w_pw2, bl, b_pw2, mode="resid", xres=xp.reshape(B * T, D), gate=mp[2],
                        rows_per_gate=tm_rows).reshape(B, T, D)
            new_cp.append(u[:, T - (CONV_WIDTH - 1):])
            u_s = matmul(hs, w_pw1, bl, b_pw1, mode="glu", n_out=D)
            st = state_conv[bl]
            y_s = conv_sample(st.transpose(1, 0, 2), u_s[:S], w_dw, b_dw, bl)
            z_s = ln_swish(jnp.pad(y_s, ((0, SAMPLE_PAD - S), (0, 0))), ln_g, ln_b, bl)
            xs = matmul(z_s, w_pw2, bl, b_pw2, mode="resid", xres=xs.reshape(SAMPLE_PAD, D),
                        gate=ms[2]).reshape(1, SAMPLE_PAD, D)
            new_cs.append(jnp.concatenate([st[:, 1:], u_s[:S, None]], axis=1))
        xp = _peer_layer(xp, g_ffn[i], mp[4], mp[3], mp[5], w_pq, sub_keys, u_tab, v_tab, i)
        xs = _peer_layer(xs, g_ffn[i], ms[4], ms[3], ms[5], w_pq, sub_keys, u_tab, v_tab, i)

    y_prompt = norm_mod(xp, g_final, out_dtype=F32)
    y_sample = norm_mod(xs, g_final, out_dtype=F32)[0, :S].reshape(S, 1, D)
    return (y_prompt, y_sample, jnp.stack(new_kp), jnp.stack(new_vp), jnp.stack(new_cp),
            jnp.stack(new_ks), jnp.stack(new_vs), jnp.stack(new_cs))
```

```python
import functools
import math

import jax
import jax.numpy as jnp
import numpy as np
from jax import lax
from jax.experimental import pallas as pl
from jax.experimental.pallas import tpu as pltpu

D_MODEL = 2048
DEPTH = 4
N_HEADS = 64
N_KV_HEADS = 8
HEAD_DIM = 64
GROUP = N_HEADS // N_KV_HEADS
Q_DIM = N_HEADS * HEAD_DIM
KV_DIM = N_KV_HEADS * HEAD_DIM
WINDOW = 128
NUM_BUCKETS = 32
MAX_DISTANCE = 128
CONV_WIDTH = 31
PEER_HEADS = 8
N_KEYS = 128
N_EXPERTS = N_KEYS * N_KEYS
PEER_TOPK = 16
D_KEY_HALF = 128
EPS = 1e-6
NEG = -1e30
LOG2E = 1.4426950408889634

VMEM_LIMIT_V7X = 56 * 1024 * 1024
SAMPLE_PAD = 128

BF16 = jnp.bfloat16
F32 = jnp.float32


def _cparams(sem, flags=None):
    return pltpu.CompilerParams(dimension_semantics=sem, vmem_limit_bytes=VMEM_LIMIT_V7X, flags=flags)


def _dot(a, b):
    return jnp.dot(a, b, preferred_element_type=F32)


def _dot_nt(a, b):
    return lax.dot_general(a, b, (((1,), (1,)), ((), ())), preferred_element_type=F32)


def _dot_tn(a, b):
    return lax.dot_general(a, b, (((0,), (0,)), ((), ())), preferred_element_type=F32)


def _ada_kernel(c_ref, w_ref, b_ref, o_ref):
    c = c_ref[...]
    cond = (c * jax.nn.sigmoid(c)).astype(BF16)
    o_ref[...] = _dot(cond, w_ref[...].astype(BF16)) + b_ref[...]


def ada_mod(c_all, w_ada, b_ada):
    rows = c_all.shape[0]
    n = w_ada.shape[2]
    tn = 1536
    return pl.pallas_call(
        _ada_kernel,
        grid=(DEPTH, n // tn),
        in_specs=[
            pl.BlockSpec((rows, D_MODEL), lambda l, j: (0, 0)),
            pl.BlockSpec((None, D_MODEL, tn), lambda l, j: (l, 0, j)),
            pl.BlockSpec((None, 1, tn), lambda l, j: (l, 0, j)),
        ],
        out_specs=pl.BlockSpec((None, rows, tn), lambda l, j: (l, 0, j)),
        out_shape=jax.ShapeDtypeStruct((DEPTH, rows, n), F32),
        compiler_params=_cparams(("arbitrary", "arbitrary")),
        name="ada_mod",
    )(c_all, w_ada, b_ada.reshape(DEPTH, 1, n))


def _norm_kernel(*refs, modulated, transposed):
    if modulated:
        x_ref, g_ref, sc_ref, sh_ref = refs[:4]
        outs = refs[4:]
    else:
        x_ref, g_ref = refs[:2]
        outs = refs[2:]
    x = x_ref[0]
    y = x * lax.rsqrt(jnp.mean(x * x, axis=-1, keepdims=True) + EPS) * g_ref[...]
    if modulated:
        y = y * (1.0 + sc_ref[0]) + sh_ref[0]
    outs[0][0] = y.astype(outs[0].dtype)
    if transposed:
        outs[1][...] = y.T.astype(outs[1].dtype)


def norm_mod(x, g, sc=None, sh=None, *, out_dtype=BF16, transposed=False):
    G, R, D = x.shape
    tt = min(R, 512)
    nt = R // tt
    modulated = sc is not None
    in_specs = [
        pl.BlockSpec((1, tt, D), lambda b, i: (b, i, 0)),
        pl.BlockSpec((1, D), lambda b, i: (0, 0)),
    ]
    args = [x, g.reshape(1, D)]
    if modulated:
        rm = sc.shape[1]
        if rm == 1:
            mspec = pl.BlockSpec((1, 1, D), lambda b, i: (b, 0, 0))
        else:
            mspec = pl.BlockSpec((1, tt, D), lambda b, i: (b, i, 0))
        in_specs += [mspec, mspec]
        args += [sc, sh]
    out_specs = [pl.BlockSpec((1, tt, D), lambda b, i: (b, i, 0))]
    out_shape = [jax.ShapeDtypeStruct((G, R, D), out_dtype)]
    if transposed:
        out_specs.append(pl.BlockSpec((D, tt), lambda b, i: (0, b * nt + i)))
        out_shape.append(jax.ShapeDtypeStruct((D, G * R), BF16))
    res = pl.pallas_call(
        functools.partial(_norm_kernel, modulated=modulated, transposed=transposed),
        grid=(G, nt),
        in_specs=in_specs,
        out_specs=out_specs,
        out_shape=out_shape,
        compiler_params=_cparams(("arbitrary", "arbitrary")),
        name="norm_mod",
    )(*args)
    return res if transposed else res[0]


def _mm_kernel(*refs, mode, has_bias):
    it = iter(refs)
    h_ref = next(it)
    w_ref = next(it)
    w2_ref = next(it) if mode == "glu" else None
    b_ref = next(it) if has_bias else None
    b2_ref = next(it) if (mode == "glu" and has_bias) else None
    if mode == "resid":
        x_ref = next(it)
        gate_ref = next(it)
    o_ref = next(it)
    ws_ref = next(it)
    ws2_ref = next(it) if mode == "glu" else None

    @pl.when(pl.program_id(1) == 0)
    def _():
        ws_ref[...] = w_ref[...].astype(BF16)
        if mode == "glu":
            ws2_ref[...] = w2_ref[...].astype(BF16)

    h = h_ref[...]
    acc = _dot(h, ws_ref[...])
    if has_bias:
        acc = acc + b_ref[...]
    if mode == "glu":
        gte = _dot(h, ws2_ref[...])
        if has_bias:
            gte = gte + b2_ref[...]
        acc = acc * jax.nn.sigmoid(gte)
    if mode == "resid":
        acc = x_ref[...] + gate_ref[0] * acc
    o_ref[...] = acc.astype(o_ref.dtype)


def matmul(h, w, layer, bias=None, *, mode="plain", n_out=None, xres=None, gate=None,
           rows_per_gate=None, out_dtype=F32, tn=512):
    M, K = h.shape
    nw = w.shape[2]
    N = n_out if n_out is not None else nw
    tm = min(M, 1024)
    nj, ni = N // tn, M // tm
    has_bias = bias is not None
    in_specs = [
        pl.BlockSpec((tm, K), lambda j, i: (i, 0)),
        pl.BlockSpec((None, K, tn), lambda j, i: (layer, 0, j)),
    ]
    args = [h, w]
    if mode == "glu":
        in_specs.append(pl.BlockSpec((None, K, tn), lambda j, i: (layer, 0, j + nj)))
        args.append(w)
    if has_bias:
        b3 = bias.reshape(bias.shape[0], 1, nw)
        in_specs.append(pl.BlockSpec((None, 1, tn), lambda j, i: (layer, 0, j)))
        args.append(b3)
        if mode == "glu":
            in_specs.append(pl.BlockSpec((None, 1, tn), lambda j, i: (layer, 0, j + nj)))
            args.append(b3)
    if mode == "resid":
        in_specs.append(pl.BlockSpec((tm, tn), lambda j, i: (i, j)))
        args.append(xres)
        if gate.shape[1] == 1:
            tiles_per_gate = rows_per_gate // tm
            in_specs.append(pl.BlockSpec((1, 1, tn), lambda j, i: (i // tiles_per_gate, 0, j)))
        else:
            in_specs.append(pl.BlockSpec((1, tm, tn), lambda j, i: (0, i, j)))
        args.append(gate)
    scratch = [pltpu.VMEM((K, tn), BF16)]
    if mode == "glu":
        scratch.append(pltpu.VMEM((K, tn), BF16))
    return pl.pallas_call(
        functools.partial(_mm_kernel, mode=mode, has_bias=has_bias),
        grid=(nj, ni),
        in_specs=in_specs,
        out_specs=pl.BlockSpec((tm, tn), lambda j, i: (i, j)),
        out_shape=jax.ShapeDtypeStruct((M, N), out_dtype),
        scratch_shapes=scratch,
        compiler_params=_cparams(("arbitrary", "arbitrary")),
        name="proj_" + mode,
    )(*args)


def _resid_kernel(x_ref, y_ref, g_ref, o_ref):
    o_ref[0] = x_ref[0] + g_ref[0] * y_ref[0]


def residual(x, y, gate):
    G, R, D = x.shape
    tt = min(R, 512)
    if gate.shape[1] == 1:
        gspec = pl.BlockSpec((1, 1, D), lambda b, i: (b, 0, 0))
    else:
        gspec = pl.BlockSpec((1, tt, D), lambda b, i: (b, i, 0))
    spec = pl.BlockSpec((1, tt, D), lambda b, i: (b, i, 0))
    return pl.pallas_call(
        _resid_kernel,
        grid=(G, R // tt),
        in_specs=[spec, spec, gspec],
        out_specs=spec,
        out_shape=jax.ShapeDtypeStruct((G, R, D), F32),
        compiler_params=_cparams(("arbitrary", "arbitrary")),
        name="residual",
    )(x, y, gate)


PAIRS = GROUP // 2


def _attn_prompt_kernel(q_ref, kc_ref, kp_ref, vc_ref, vp_ref, bias_ref, o_ref,
                        *, blocks_per_seq):
    which = ((pl.program_id(0) % blocks_per_seq) == 0).astype(jnp.int32)
    zeros = jnp.zeros((2 * WINDOW, HEAD_DIM), BF16)
    ones = jnp.ones((2 * WINDOW, 2 * HEAD_DIM), BF16)
    not_sink = lax.broadcasted_iota(jnp.int32, (2 * WINDOW, HEAD_DIM), 0) > 0
    for g in range(N_KV_HEADS):
        ks = slice(g * HEAD_DIM, (g + 1) * HEAD_DIM)
        kcat = jnp.concatenate([kp_ref[:, ks], kc_ref[:, ks]], axis=0)
        vcat = jnp.concatenate([vp_ref[:, ks], vc_ref[:, ks]], axis=0)
        kcat = jnp.where(not_sink, kcat, 0.0).astype(BF16)
        vcat = jnp.where(not_sink, vcat, 0.0).astype(BF16)
        qs = slice(g * GROUP * HEAD_DIM, (g + 1) * GROUP * HEAD_DIM)
        qg = jnp.concatenate(
            [q_ref[:, qs.start + pp * 2 * HEAD_DIM:qs.start + (pp + 1) * 2 * HEAD_DIM] for pp in range(PAIRS)],
            axis=0)
        qg = (qg * (HEAD_DIM ** -0.5 * LOG2E)).astype(BF16)
        out = None
        for par in range(2):
            kx = jnp.concatenate([kcat, zeros] if par == 0 else [zeros, kcat], axis=1)
            vx = jnp.concatenate(([vcat, zeros] if par == 0 else [zeros, vcat]) + [ones], axis=1)
            s = _dot_nt(qg, kx) + bias_ref[which, g, par]
            p = jnp.exp2(s - jnp.max(s, axis=-1, keepdims=True))
            od = _dot(p.astype(BF16), vx)
            o = od[:, :2 * HEAD_DIM] / od[:, 2 * HEAD_DIM:]
            out = o if out is None else out + o
        for pp in range(PAIRS):
            o_ref[:, qs.start + pp * 2 * HEAD_DIM:qs.start + (pp + 1) * 2 * HEAD_DIM] = (
                out[pp * WINDOW:(pp + 1) * WINDOW].astype(o_ref.dtype))


def attn_prompt(qkv, bias_tab, sinks, seq_len):
    sink_rows = jnp.repeat(sinks.astype(F32).reshape(N_KV_HEADS, PAIRS, 2).transpose(0, 2, 1), WINDOW, axis=2)
    bias_tab = bias_tab.at[:, :, :, :, 0].set((sink_rows * LOG2E)[None])
    M = qkv.shape[0]
    nb = M // WINDOW
    bps = seq_len // WINDOW
    kcol = Q_DIM // KV_DIM
    prev = lambda r: jnp.maximum(r - 1, 0)
    return pl.pallas_call(
        functools.partial(_attn_prompt_kernel, blocks_per_seq=bps),
        grid=(nb,),
        in_specs=[
            pl.BlockSpec((WINDOW, Q_DIM), lambda r: (r, 0)),
            pl.BlockSpec((WINDOW, KV_DIM), lambda r: (r, kcol)),
            pl.BlockSpec((WINDOW, KV_DIM), lambda r: (prev(r), kcol)),
            pl.BlockSpec((WINDOW, KV_DIM), lambda r: (r, kcol + 1)),
            pl.BlockSpec((WINDOW, KV_DIM), lambda r: (prev(r), kcol + 1)),
            pl.BlockSpec((2, N_KV_HEADS, 2, PAIRS * WINDOW, 2 * WINDOW), lambda r: (0, 0, 0, 0, 0),
                         pipeline_mode=pl.Buffered(1)),
        ],
        out_specs=pl.BlockSpec((WINDOW, Q_DIM), lambda r: (r, 0)),
        out_shape=jax.ShapeDtypeStruct((M, Q_DIM), BF16),
        compiler_params=_cparams(("arbitrary",)),
        name="attn_prompt",
    )(qkv, qkv, qkv, qkv, qkv, bias_tab)


SAMPLE_SEQS_PER_STEP = 8


def _bdot(a, b, contract_a, contract_b):
    return lax.dot_general(a, b, (((contract_a,), (contract_b,)), ((0,), (0,))), preferred_element_type=F32)


def _attn_sample_kernel(q_ref, kn_ref, vn_ref, ck_ref, cv_ref, bias_ref, bias0_ref, sink_ref, o_ref):
    for g in range(N_KV_HEADS):
        ks = slice(g * HEAD_DIM, (g + 1) * HEAD_DIM)
        hs = slice(g * GROUP, (g + 1) * GROUP)
        qg = q_ref[:, hs, :] * (HEAD_DIM ** -0.5)
        kn = kn_ref[:, :, ks]
        vn = vn_ref[:, :, ks]
        s = _bdot(qg.astype(BF16), ck_ref[:, :, ks].astype(BF16), 2, 2) + bias_ref[hs, :][None]
        s_new = jnp.sum(qg * kn, axis=-1, keepdims=True) + bias0_ref[hs, :][None]
        sink = sink_ref[hs, :][None]
        m = jnp.maximum(jnp.maximum(jnp.max(s, axis=-1, keepdims=True), s_new), sink)
        p = jnp.exp(s - m)
        pn = jnp.exp(s_new - m)
        den = jnp.sum(p, axis=-1, keepdims=True) + pn + jnp.exp(sink - m)
        o = (_bdot(p.astype(BF16), cv_ref[:, :, ks].astype(BF16), 2, 1) + pn * vn) / den
        o_ref[:, hs, :] = o.astype(o_ref.dtype)


def attn_sample(q3, k_new, v_new, cache_k, cache_v, bias_s, bias0, sinks):
    nseq = cache_k.shape[0]
    sb = SAMPLE_SEQS_PER_STEP
    return pl.pallas_call(
        _attn_sample_kernel,
        grid=(nseq // sb,),
        in_specs=[
            pl.BlockSpec((sb, N_HEADS, HEAD_DIM), lambda b: (b, 0, 0)),
            pl.BlockSpec((sb, 1, KV_DIM), lambda b: (b, 0, 0)),
            pl.BlockSpec((sb, 1, KV_DIM), lambda b: (b, 0, 0)),
            pl.BlockSpec((sb, WINDOW, KV_DIM), lambda b: (b, 0, 0)),
            pl.BlockSpec((sb, WINDOW, KV_DIM), lambda b: (b, 0, 0)),
            pl.BlockSpec((N_HEADS, WINDOW), lambda b: (0, 0)),
            pl.BlockSpec((N_HEADS, 1), lambda b: (0, 0)),
            pl.BlockSpec((N_HEADS, 1), lambda b: (0, 0)),
        ],
        out_specs=pl.BlockSpec((sb, N_HEADS, HEAD_DIM), lambda b: (b, 0, 0)),
        out_shape=jax.ShapeDtypeStruct((nseq, N_HEADS, HEAD_DIM), F32),
        compiler_params=_cparams(("arbitrary",)),
        name="attn_sample",
    )(q3, k_new, v_new, cache_k, cache_v, bias_s, bias0, sinks)


CONV_HALO = 32
F32_SUBLANES = 8


def _conv_prompt_kernel(u_ref, halo_ref, w_ref, b_ref, o_ref, scr_ref, *, tt):
    first = pl.program_id(1) == 0

    @pl.when(first)
    def _():
        scr_ref[0:CONV_HALO, :] = jnp.zeros((CONV_HALO, scr_ref.shape[1]), F32)

    @pl.when(jnp.logical_not(first))
    def _():
        scr_ref[0:CONV_HALO, :] = halo_ref[0]

    scr_ref[CONV_HALO:, :] = u_ref[0]
    off = CONV_HALO - (CONV_WIDTH - 1)
    acc = jnp.broadcast_to(b_ref[...], (tt, scr_ref.shape[1]))
    for s in range(F32_SUBLANES):
        rows = tt if s == 0 else tt + F32_SUBLANES
        part = None
        for q in range(off, off + CONV_WIDTH):
            if q % F32_SUBLANES != s:
                continue
            term = scr_ref[q - s:q - s + rows, :] * w_ref[q - off:q - off + 1, :]
            part = term if part is None else part + term
        acc = acc + (part if s == 0 else part[s:s + tt])
    o_ref[0] = acc


def conv_prompt(u, w_dw, b_dw, layer):
    B, T, D = u.shape
    tt, dc = 512, 512
    hb = tt // CONV_HALO
    return pl.pallas_call(
        functools.partial(_conv_prompt_kernel, tt=tt),
        grid=(B, T // tt, D // dc),
        in_specs=[
            pl.BlockSpec((1, tt, dc), lambda b, i, j: (b, i, j)),
            pl.BlockSpec((1, CONV_HALO, dc), lambda b, i, j: (b, jnp.maximum(i * hb - 1, 0), j)),
            pl.BlockSpec((None, CONV_WIDTH, dc), lambda b, i, j: (layer, 0, j)),
            pl.BlockSpec((None, 1, dc), lambda b, i, j: (layer, 0, j)),
        ],
        out_specs=pl.BlockSpec((1, tt, dc), lambda b, i, j: (b, i, j)),
        out_shape=jax.ShapeDtypeStruct((B, T, D), F32),
        scratch_shapes=[pltpu.VMEM((CONV_HALO + tt, dc), F32)],
        compiler_params=_cparams(("arbitrary", "arbitrary", "arbitrary")),
        name="conv_prompt",
    )(u, u, w_dw, b_dw.reshape(b_dw.shape[0], 1, D))


def _conv_sample_kernel(st_ref, u_ref, w_ref, b_ref, o_ref):
    acc = u_ref[...] * w_ref[CONV_WIDTH - 1:CONV_WIDTH, :] + b_ref[...]
    for w in range(CONV_WIDTH - 1):
        acc = acc + st_ref[w] * w_ref[w:w + 1, :]
    o_ref[...] = acc


def conv_sample(state_t, u, w_dw, b_dw, layer):
    nseq, D = u.shape
    return pl.pallas_call(
        _conv_sample_kernel,
        grid=(1,),
        in_specs=[
            pl.BlockSpec((CONV_WIDTH - 1, nseq, D), lambda i: (0, 0, 0)),
            pl.BlockSpec((nseq, D), lambda i: (0, 0)),
            pl.BlockSpec((None, CONV_WIDTH, D), lambda i: (layer, 0, 0)),
            pl.BlockSpec((None, 1, D), lambda i: (layer, 0, 0)),
        ],
        out_specs=pl.BlockSpec((nseq, D), lambda i: (0, 0)),
        out_shape=jax.ShapeDtypeStruct((nseq, D), F32),
        compiler_params=_cparams(("arbitrary",)),
        name="conv_sample",
    )(state_t, u, w_dw, b_dw.reshape(b_dw.shape[0], 1, D))


def _ln_swish_kernel(y_ref, g_ref, b_ref, o_ref):
    y = y_ref[...]
    mu = jnp.mean(y, axis=-1, keepdims=True)
    yc = y - mu
    var = jnp.mean(yc * yc, axis=-1, keepdims=True)
    z = yc * lax.rsqrt(var + EPS) * g_ref[...] + b_ref[...]
    o_ref[...] = (z * jax.nn.sigmoid(z)).astype(o_ref.dtype)


def ln_swish(y, ln_g, ln_b, layer):
    M, D = y.shape
    tt = min(M, 512)
    return pl.pallas_call(
        _ln_swish_kernel,
        grid=(M // tt,),
        in_specs=[
            pl.BlockSpec((tt, D), lambda i: (i, 0)),
            pl.BlockSpec((None, 1, D), lambda i: (layer, 0, 0)),
            pl.BlockSpec((None, 1, D), lambda i: (layer, 0, 0)),
        ],
        out_specs=pl.BlockSpec((tt, D), lambda i: (i, 0)),
        out_shape=jax.ShapeDtypeStruct((M, D), BF16),
        compiler_params=_cparams(("arbitrary",)),
        name="ln_swish",
    )(y, ln_g.reshape(-1, 1, D), ln_b.reshape(-1, 1, D))


NOT_RANKED = 127.0


def _top_values(s, k, with_rank=False):
    out = []
    rank = jnp.full(s.shape, NOT_RANKED, F32) if with_rank else None
    for i in range(k):
        m = jnp.max(s, axis=0, keepdims=True)
        out.append(m)
        hit = s == m
        if with_rank:
            rank = jnp.where(hit, float(i), rank)
        s = jnp.where(hit, -jnp.inf, s)
    return out, rank


def _peer_topk_kernel(q_ref, keys_ref, n_ref, a_ref, r1_ref, e1_ref, top_ref):
    for h in range(PEER_HEADS):
        halves = []
        for p in range(2):
            hp = 2 * h + p
            qs = q_ref[:, hp * D_KEY_HALF:(hp + 1) * D_KEY_HALF].astype(BF16)
            s = _dot_nt(keys_ref[hp].astype(BF16), qs)
            tops, rank = _top_values(s, PEER_TOPK, with_rank=(p == 1))
            for k, m in enumerate(tops):
                top_ref[p, k:k + 1, :] = m
            halves.append((s, tops, rank))
        (s0, t0, _), (s1, t1, rank1) = halves
        t1_all = top_ref[1]
        pieces = ([t0[0] + t1_all] + [t0[a] + t1_all[0:8] for a in range(1, 8)]
                  + [top_ref[0, 8:PEER_TOPK, :] + t1[0]])
        best, _ = _top_values(jnp.concatenate(pieces, axis=0), PEER_TOPK)
        tau = best[-1]
        z = jnp.exp(best[0] - best[0])
        for v in best[1:]:
            z = z + jnp.exp(v - best[0])
        n = jnp.zeros(s0.shape, F32)
        for a in range(PEER_TOPK):
            cnt = jnp.sum(jnp.where(t0[a] + t1_all >= tau, 1.0, 0.0), axis=0, keepdims=True)
            n = jnp.where(s0 == t0[a], cnt, n)
        n_ref[h] = n
        a_ref[h] = jnp.exp(s0 - t0[0])
        r1_ref[h] = rank1.astype(BF16)
        e1_ref[h] = (jnp.exp(s1 - t1[0]) / z).astype(BF16)


def peer_topk(q, sub_keys, layer):
    M = q.shape[0]
    tt = min(M, 256)
    keys = sub_keys.reshape(DEPTH, PEER_HEADS * 2, N_KEYS, D_KEY_HALF)
    return pl.pallas_call(
        _peer_topk_kernel,
        grid=(M // tt,),
        in_specs=[
            pl.BlockSpec((tt, PEER_HEADS * 2 * D_KEY_HALF), lambda i: (i, 0)),
            pl.BlockSpec((None, PEER_HEADS * 2, N_KEYS, D_KEY_HALF), lambda i: (layer, 0, 0, 0)),
        ],
        out_specs=[pl.BlockSpec((PEER_HEADS, N_KEYS, tt), lambda i: (0, 0, i))] * 4,
        out_shape=[
            jax.ShapeDtypeStruct((PEER_HEADS, N_KEYS, M), F32),
            jax.ShapeDtypeStruct((PEER_HEADS, N_KEYS, M), F32),
            jax.ShapeDtypeStruct((PEER_HEADS, N_KEYS, M), BF16),
            jax.ShapeDtypeStruct((PEER_HEADS, N_KEYS, M), BF16),
        ],
        scratch_shapes=[pltpu.VMEM((2, PEER_TOPK, tt), F32)],
        compiler_params=_cparams(("arbitrary",)),
        name="peer_topk",
    )(q, keys)


PEER_EC = 512


BF16_SUBLANES = 16


N_CHUNKS = N_EXPERTS // PEER_EC
GATE_TOKENS = 256


def _bcast_row_bf16(ref, h, i, ts):
    row = ref[h, pl.ds(i, 1), ts]
    rep = jnp.broadcast_to(row, (F32_SUBLANES, row.shape[1]))
    return jnp.concatenate([rep, rep], axis=0).astype(BF16)


def _peer_dense_kernel(hT_ref, n_ref, a_ref, r1_ref, e1_ref, u_ref, v_ref, o_ref,
                       act_ref, w_ref, ub_ref, vb_ref):
    e = pl.program_id(1)
    tt = hT_ref.shape[1]

    @pl.when(e == 0)
    def _():
        o_ref[...] = jnp.zeros(o_ref.shape, F32)

    ub_ref[...] = u_ref[...].astype(BF16)
    vb_ref[...] = v_ref[...].astype(BF16)
    act_ref[...] = _dot(ub_ref[...], hT_ref[...])
    gate_tokens = min(GATE_TOKENS, tt)
    for tp in range(tt // gate_tokens):
        ts = slice(tp * gate_tokens, (tp + 1) * gate_tokens)
        for ii in range(PEER_EC // N_KEYS):
            i = e * (PEER_EC // N_KEYS) + ii
            g = None
            for h in range(PEER_HEADS):
                nrow = _bcast_row_bf16(n_ref, h, i, ts)
                arow = _bcast_row_bf16(a_ref, h, i, ts)
                gate = e1_ref[h, :, :, ts] * arow[None]
                val = jnp.where(r1_ref[h, :, :, ts] < nrow[None], gate, jnp.zeros_like(gate))
                g = val if g is None else g + val
            es = slice(ii * N_KEYS, (ii + 1) * N_KEYS)
            act = act_ref[es, ts]
            ge = (0.5 * act * (1.0 + lax.erf(act * (1.0 / math.sqrt(2.0))))).astype(BF16)
            w_ref[ts, es] = (ge * g.reshape(N_KEYS, gate_tokens)).T
    o_ref[...] += _dot(w_ref[...], vb_ref[...])


def peer_dense(hT, n, a, r1, e1, u_tab, v_tab, layer):
    D, M = hT.shape
    tt = min(M, 1024)
    groups = N_KEYS // BF16_SUBLANES
    r1 = r1.reshape(PEER_HEADS, groups, BF16_SUBLANES, M)
    e1 = e1.reshape(PEER_HEADS, groups, BF16_SUBLANES, M)
    once = pl.Buffered(1)
    return pl.pallas_call(
        _peer_dense_kernel,
        grid=(M // tt, N_CHUNKS),
        in_specs=[
            pl.BlockSpec((D, tt), lambda t, e: (0, t), pipeline_mode=once),
            pl.BlockSpec((PEER_HEADS, N_KEYS, tt), lambda t, e: (0, 0, t), pipeline_mode=once),
            pl.BlockSpec((PEER_HEADS, N_KEYS, tt), lambda t, e: (0, 0, t), pipeline_mode=once),
            pl.BlockSpec((PEER_HEADS, groups, BF16_SUBLANES, tt), lambda t, e: (0, 0, 0, t), pipeline_mode=once),
            pl.BlockSpec((PEER_HEADS, groups, BF16_SUBLANES, tt), lambda t, e: (0, 0, 0, t), pipeline_mode=once),
            pl.BlockSpec((None, PEER_EC, D), lambda t, e: (layer, e, 0)),
            pl.BlockSpec((None, PEER_EC, D), lambda t, e: (layer, e, 0)),
        ],
        out_specs=pl.BlockSpec((tt, D), lambda t, e: (t, 0), pipeline_mode=once),
        out_shape=jax.ShapeDtypeStruct((M, D), F32),
        scratch_shapes=[pltpu.VMEM((PEER_EC, tt), F32), pltpu.VMEM((tt, PEER_EC), BF16),
                        pltpu.VMEM((PEER_EC, D), BF16), pltpu.VMEM((PEER_EC, D), BF16)],
        compiler_params=_cparams(("arbitrary", "arbitrary")),
        name="peer_dense",
    )(hT, n, a, r1, e1, u_tab, v_tab)


def _t5_bucket(rel):
    n = jnp.maximum(rel, 0)
    max_exact = NUM_BUCKETS // 2
    nf = jnp.maximum(n, 1).astype(F32)
    large = max_exact + (jnp.log(nf / max_exact) / math.log(MAX_DISTANCE / max_exact)
                         * (NUM_BUCKETS - max_exact)).astype(jnp.int32)
    large = jnp.minimum(large, NUM_BUCKETS - 1)
    return jnp.where(n < max_exact, n, large)


def _bias_tables(rel_bias):
    rb = rel_bias.astype(F32)
    per_rel = rb[_t5_bucket(jnp.arange(WINDOW, dtype=jnp.int32))].T
    span = 3 * WINDOW - 1
    line = jnp.full((N_HEADS, span), NEG, F32).at[:, 1:WINDOW + 1].set(per_rel[:, ::-1])
    skew = jnp.tile(line, (1, WINDOW))[:, :WINDOW * (span - 1)].reshape(N_HEADS, WINDOW, span - 1)
    tab = skew[:, :, :2 * WINDOW]
    bias_p = (tab.reshape(N_KV_HEADS, PAIRS, 2, WINDOW, 2 * WINDOW)
              .transpose(0, 2, 1, 3, 4).reshape(N_KV_HEADS, 2, PAIRS * WINDOW, 2 * WINDOW))
    bias_p = bias_p * LOG2E
    no_prev = jnp.arange(2 * WINDOW) < WINDOW
    bias_p = jnp.stack([bias_p, jnp.where(no_prev, NEG * LOG2E, bias_p)])
    rel_s = WINDOW - jnp.arange(WINDOW, dtype=jnp.int32)
    tab_s = jnp.where((rel_s < WINDOW)[:, None], rb[_t5_bucket(rel_s)], NEG)
    bias_s = tab_s.T
    bias0 = rb[_t5_bucket(jnp.zeros((1,), jnp.int32))].T
    return bias_p, bias_s, bias0


def _peer_layer(x, g_ffn_i, sc, sh, gt, w_pq, sub_keys, u_tab, v_tab, i):
    G, R, D = x.shape
    h, hT = norm_mod(x, g_ffn_i, sc, sh, transposed=True)
    q = matmul(h.reshape(G * R, D), w_pq, i)
    n, a, r1, e1 = peer_topk(q, sub_keys, i)
    y = peer_dense(hT, n, a, r1, e1, u_tab, v_tab, i)
    return residual(x, y.reshape(G, R, D), gt)


def _split_mod(mod):
    return [mod[:, :, k * D_MODEL:(k + 1) * D_MODEL] for k in range(6)]


def kernel(x_prompt, x_sample, cache_k, cache_v, state_conv, c_prompt, c_sample, rel_bias,
           w_ada, b_ada, g_mix, g_ffn, g_final, w_qkv, b_qkv, w_o, sinks,
           w_pw1, b_pw1, w_dw, b_dw, ln_g, ln_b, w_pw2, b_pw2, w_pq, sub_keys, u_tab, v_tab):
    B, T, D = x_prompt.shape
    S = x_sample.shape[0]
    n_attn = w_qkv.shape[0]

    rows = B + S
    rows_pad = -(-rows // 16) * 16
    c_all = jnp.concatenate([c_prompt, c_sample, jnp.zeros((rows_pad - rows, D), F32)], axis=0)
    mod = ada_mod(c_all, w_ada, b_ada)
    bias_p, bias_s, bias0 = _bias_tables(rel_bias)

    xp = x_prompt
    xs = jnp.pad(x_sample.reshape(1, S, D), ((0, 0), (0, SAMPLE_PAD - S), (0, 0)))
    new_kp, new_vp, new_cp, new_ks, new_vs, new_cs = [], [], [], [], [], []
    tm_rows = T

    for i in range(DEPTH):
        mp = _split_mod(mod[i, :B].reshape(B, 1, 6 * D))
        ms = _split_mod(jnp.pad(mod[i, B:B + S], ((0, SAMPLE_PAD - S), (0, 0))).reshape(1, SAMPLE_PAD, 6 * D))
        hp = norm_mod(xp, g_mix[i], mp[1], mp[0]).reshape(B * T, D)
        hs = norm_mod(xs, g_mix[i], ms[1], ms[0]).reshape(SAMPLE_PAD, D)
        if i % 2 == 0:
            a = i // 2
            sink_a = sinks[a].astype(F32)
            qkv = matmul(hp, w_qkv, a, b_qkv)
            o = attn_prompt(qkv, bias_p, sink_a, T)
            xp = matmul(o, w_o, a, mode="resid", xres=xp.reshape(B * T, D), gate=mp[2],
                        rows_per_gate=tm_rows).reshape(B, T, D)
            kv = qkv.reshape(B, T, Q_DIM + 2 * KV_DIM)[:, T - WINDOW:, Q_DIM:]
            new_kp.append(kv[..., :KV_DIM].reshape(B, WINDOW, N_KV_HEADS, HEAD_DIM))
            new_vp.append(kv[..., KV_DIM:].reshape(B, WINDOW, N_KV_HEADS, HEAD_DIM))
            qkv_s = matmul(hs, w_qkv, a, b_qkv)
            q3 = qkv_s[:S, :Q_DIM].reshape(S, N_HEADS, HEAD_DIM)
            k_new = qkv_s[:S, Q_DIM:Q_DIM + KV_DIM]
            v_new = qkv_s[:S, Q_DIM + KV_DIM:]
            ck = cache_k[a].reshape(S, WINDOW, KV_DIM)
            cv = cache_v[a].reshape(S, WINDOW, KV_DIM)
            o_s = attn_sample(q3, k_new.reshape(S, 1, KV_DIM), v_new.reshape(S, 1, KV_DIM), ck, cv,
                              bias_s, bias0, sink_a.reshape(N_HEADS, 1))
            o_s = jnp.pad(o_s.reshape(S, Q_DIM), ((0, SAMPLE_PAD - S), (0, 0))).astype(BF16)
            xs = matmul(o_s, w_o, a, mode="resid",
                        xres=xs.reshape(SAMPLE_PAD, D), gate=ms[2]).reshape(1, SAMPLE_PAD, D)
            new_ks.append(jnp.concatenate([ck[:, 1:], k_new[:, None]], axis=1)
                          .reshape(S, WINDOW, N_KV_HEADS, HEAD_DIM))
            new_vs.append(jnp.concatenate([cv[:, 1:], v_new[:, None]], axis=1)
                          .reshape(S, WINDOW, N_KV_HEADS, HEAD_DIM))
        else:
            bl = i // 2
            u = matmul(hp, w_pw1, bl, b_pw1, mode="glu", n_out=D).reshape(B, T, D)
            y = conv_prompt(u, w_dw, b_dw, bl)
            z = ln_swish(y.reshape(B * T, D), ln_g, ln_b, bl)
            xp = matmul(z, w_pw2, bl, b_pw2, mode="resid", xres=xp.reshape(B * T, D), gate=mp[2],
                        rows_per_gate=tm_rows).reshape(B, T, D)
            new_cp.append(u[:, T - (CONV_WIDTH - 1):])
            u_s = matmul(hs, w_pw1, bl, b_pw1, mode="glu", n_out=D)
            st = state_conv[bl]
            y_s = conv_sample(st.transpose(1, 0, 2), u_s[:S], w_dw, b_dw, bl)
            z_s = ln_swish(jnp.pad(y_s, ((0, SAMPLE_PAD - S), (0, 0))), ln_g, ln_b, bl)
            xs = matmul(z_s, w_pw2, bl, b_pw2, mode="resid", xres=xs.reshape(SAMPLE_PAD, D),
                        gate=ms[2]).reshape(1, SAMPLE_PAD, D)
            new_cs.append(jnp.concatenate([st[:, 1:], u_s[:S, None]], axis=1))
        xp = _peer_layer(xp, g_ffn[i], mp[4], mp[3], mp[5], w_pq, sub_keys, u_tab, v_tab, i)
        xs = _peer_layer(xs, g_ffn[i], ms[4], ms[3], ms[5], w_pq, sub_keys, u_tab, v_tab, i)

    y_prompt = norm_mod(xp, g_final, out_dtype=F32)
    y_sample = norm_mod(xs, g_final, out_dtype=F32)[0, :S].reshape(S, 1, D)
    return (y_prompt, y_sample, jnp.stack(new_kp), jnp.stack(new_vp), jnp.stack(new_cp),
            jnp.stack(new_ks), jnp.stack(new_vs), jnp.stack(new_cs))
```

```python
import functools
import math

import jax
import jax.numpy as jnp
import numpy as np
from jax import lax
from jax.experimental import pallas as pl
from jax.experimental.pallas import tpu as pltpu

D_MODEL = 2048
DEPTH = 4
N_HEADS = 64
N_KV_HEADS = 8
HEAD_DIM = 64
GROUP = N_HEADS // N_KV_HEADS
Q_DIM = N_HEADS * HEAD_DIM
KV_DIM = N_KV_HEADS * HEAD_DIM
WINDOW = 128
NUM_BUCKETS = 32
MAX_DISTANCE = 128
CONV_WIDTH = 31
PEER_HEADS = 8
N_KEYS = 128
N_EXPERTS = N_KEYS * N_KEYS
PEER_TOPK = 16
D_KEY_HALF = 128
EPS = 1e-6
NEG = -1e30
LOG2E = 1.4426950408889634

VMEM_LIMIT_V7X = 56 * 1024 * 1024
SAMPLE_PAD = 128

BF16 = jnp.bfloat16
F32 = jnp.float32


def _cparams(sem, flags=None):
    return pltpu.CompilerParams(dimension_semantics=sem, vmem_limit_bytes=VMEM_LIMIT_V7X, flags=flags)


def _dot(a, b):
    return jnp.dot(a, b, preferred_element_type=F32)


def _dot_mixed(a, b):
    return lax.dot_general(a, b, (((1,), (0,)), ((), ())), preferred_element_type=F32)


def _dot_nt(a, b):
    return lax.dot_general(a, b, (((1,), (1,)), ((), ())), preferred_element_type=F32)


def _dot_tn(a, b):
    return lax.dot_general(a, b, (((0,), (0,)), ((), ())), preferred_element_type=F32)


def _ada_kernel(c_ref, w_ref, b_ref, o_ref):
    c = c_ref[...]
    cond = (c * jax.nn.sigmoid(c)).astype(BF16)
    o_ref[...] = _dot(cond, w_ref[...].astype(BF16)) + b_ref[...]


def ada_mod(c_all, w_ada, b_ada):
    rows = c_all.shape[0]
    n = w_ada.shape[2]
    tn = 1536
    return pl.pallas_call(
        _ada_kernel,
        grid=(DEPTH, n // tn),
        in_specs=[
            pl.BlockSpec((rows, D_MODEL), lambda l, j: (0, 0)),
            pl.BlockSpec((None, D_MODEL, tn), lambda l, j: (l, 0, j)),
            pl.BlockSpec((None, 1, tn), lambda l, j: (l, 0, j)),
        ],
        out_specs=pl.BlockSpec((None, rows, tn), lambda l, j: (l, 0, j)),
        out_shape=jax.ShapeDtypeStruct((DEPTH, rows, n), F32),
        compiler_params=_cparams(("arbitrary", "arbitrary")),
        name="ada_mod",
    )(c_all, w_ada, b_ada.reshape(DEPTH, 1, n))


def _norm_kernel(*refs, modulated, transposed):
    if modulated:
        x_ref, g_ref, sc_ref, sh_ref = refs[:4]
        outs = refs[4:]
    else:
        x_ref, g_ref = refs[:2]
        outs = refs[2:]
    x = x_ref[0]
    y = x * lax.rsqrt(jnp.mean(x * x, axis=-1, keepdims=True) + EPS) * g_ref[...]
    if modulated:
        y = y * (1.0 + sc_ref[0]) + sh_ref[0]
    outs[0][0] = y.astype(outs[0].dtype)
    if transposed:
        outs[1][...] = y.T.astype(outs[1].dtype)


def norm_mod(x, g, sc=None, sh=None, *, out_dtype=BF16, transposed=False):
    G, R, D = x.shape
    tt = min(R, 512)
    nt = R // tt
    modulated = sc is not None
    in_specs = [
        pl.BlockSpec((1, tt, D), lambda b, i: (b, i, 0)),
        pl.BlockSpec((1, D), lambda b, i: (0, 0)),
    ]
    args = [x, g.reshape(1, D)]
    if modulated:
        rm = sc.shape[1]
        if rm == 1:
            mspec = pl.BlockSpec((1, 1, D), lambda b, i: (b, 0, 0))
        else:
            mspec = pl.BlockSpec((1, tt, D), lambda b, i: (b, i, 0))
        in_specs += [mspec, mspec]
        args += [sc, sh]
    out_specs = [pl.BlockSpec((1, tt, D), lambda b, i: (b, i, 0))]
    out_shape = [jax.ShapeDtypeStruct((G, R, D), out_dtype)]
    if transposed:
        out_specs.append(pl.BlockSpec((D, tt), lambda b, i: (0, b * nt + i)))
        out_shape.append(jax.ShapeDtypeStruct((D, G * R), BF16))
    res = pl.pallas_call(
        functools.partial(_norm_kernel, modulated=modulated, transposed=transposed),
        grid=(G, nt),
        in_specs=in_specs,
        out_specs=out_specs,
        out_shape=out_shape,
        compiler_params=_cparams(("arbitrary", "arbitrary")),
        name="norm_mod",
    )(*args)
    return res if transposed else res[0]


def _mm_kernel(*refs, mode, has_bias):
    it = iter(refs)
    h_ref = next(it)
    w_ref = next(it)
    w2_ref = next(it) if mode == "glu" else None
    b_ref = next(it) if has_bias else None
    b2_ref = next(it) if (mode == "glu" and has_bias) else None
    if mode == "resid":
        x_ref = next(it)
        gate_ref = next(it)
    o_ref = next(it)
    ws_ref = next(it)
    ws2_ref = next(it) if mode == "glu" else None

    @pl.when(pl.program_id(1) == 0)
    def _():
        ws_ref[...] = w_ref[...].astype(BF16)
        if mode == "glu":
            ws2_ref[...] = w2_ref[...].astype(BF16)

    h = h_ref[...]
    acc = _dot(h, ws_ref[...])
    if has_bias:
        acc = acc + b_ref[...]
    if mode == "glu":
        gte = _dot(h, ws2_ref[...])
        if has_bias:
            gte = gte + b2_ref[...]
        acc = acc * jax.nn.sigmoid(gte)
    if mode == "resid":
        acc = x_ref[...] + gate_ref[0] * acc
    o_ref[...] = acc.astype(o_ref.dtype)


def matmul(h, w, layer, bias=None, *, mode="plain", n_out=None, xres=None, gate=None,
           rows_per_gate=None, out_dtype=F32, tn=512):
    M, K = h.shape
    nw = w.shape[2]
    N = n_out if n_out is not None else nw
    tm = min(M, 1024)
    nj, ni = N // tn, M // tm
    has_bias = bias is not None
    in_specs = [
        pl.BlockSpec((tm, K), lambda j, i: (i, 0)),
        pl.BlockSpec((None, K, tn), lambda j, i: (layer, 0, j)),
    ]
    args = [h, w]
    if mode == "glu":
        in_specs.append(pl.BlockSpec((None, K, tn), lambda j, i: (layer, 0, j + nj)))
        args.append(w)
    if has_bias:
        b3 = bias.reshape(bias.shape[0], 1, nw)
        in_specs.append(pl.BlockSpec((None, 1, tn), lambda j, i: (layer, 0, j)))
        args.append(b3)
        if mode == "glu":
            in_specs.append(pl.BlockSpec((None, 1, tn), lambda j, i: (layer, 0, j + nj)))
            args.append(b3)
    if mode == "resid":
        in_specs.append(pl.BlockSpec((tm, tn), lambda j, i: (i, j)))
        args.append(xres)
        if gate.shape[1] == 1:
            tiles_per_gate = rows_per_gate // tm
            in_specs.append(pl.BlockSpec((1, 1, tn), lambda j, i: (i // tiles_per_gate, 0, j)))
        else:
            in_specs.append(pl.BlockSpec((1, tm, tn), lambda j, i: (0, i, j)))
        args.append(gate)
    scratch = [pltpu.VMEM((K, tn), BF16)]
    if mode == "glu":
        scratch.append(pltpu.VMEM((K, tn), BF16))
    return pl.pallas_call(
        functools.partial(_mm_kernel, mode=mode, has_bias=has_bias),
        grid=(nj, ni),
        in_specs=in_specs,
        out_specs=pl.BlockSpec((tm, tn), lambda j, i: (i, j)),
        out_shape=jax.ShapeDtypeStruct((M, N), out_dtype),
        scratch_shapes=scratch,
        compiler_params=_cparams(("arbitrary", "arbitrary")),
        name="proj_" + mode,
    )(*args)


def _resid_kernel(x_ref, y_ref, g_ref, o_ref):
    o_ref[0] = x_ref[0] + g_ref[0] * y_ref[0]


def residual(x, y, gate):
    G, R, D = x.shape
    tt = min(R, 512)
    if gate.shape[1] == 1:
        gspec = pl.BlockSpec((1, 1, D), lambda b, i: (b, 0, 0))
    else:
        gspec = pl.BlockSpec((1, tt, D), lambda b, i: (b, i, 0))
    spec = pl.BlockSpec((1, tt, D), lambda b, i: (b, i, 0))
    return pl.pallas_call(
        _resid_kernel,
        grid=(G, R // tt),
        in_specs=[spec, spec, gspec],
        out_specs=spec,
        out_shape=jax.ShapeDtypeStruct((G, R, D), F32),
        compiler_params=_cparams(("arbitrary", "arbitrary")),
        name="residual",
    )(x, y, gate)


PAIRS = GROUP // 2


def _attn_prompt_kernel(q_ref, kc_ref, kp_ref, vc_ref, vp_ref, bias_ref, o_ref,
                        *, blocks_per_seq):
    which = ((pl.program_id(0) % blocks_per_seq) == 0).astype(jnp.int32)
    zeros = jnp.zeros((2 * WINDOW, HEAD_DIM), BF16)
    ones = jnp.ones((2 * WINDOW, 2 * HEAD_DIM), BF16)
    not_sink = lax.broadcasted_iota(jnp.int32, (2 * WINDOW, HEAD_DIM), 0) > 0
    for g in range(N_KV_HEADS):
        ks = slice(g * HEAD_DIM, (g + 1) * HEAD_DIM)
        kcat = jnp.concatenate([kp_ref[:, ks], kc_ref[:, ks]], axis=0)
        vcat = jnp.concatenate([vp_ref[:, ks], vc_ref[:, ks]], axis=0)
        kcat = jnp.where(not_sink, kcat, 0.0).astype(BF16)
        vcat = jnp.where(not_sink, vcat, 0.0).astype(BF16)
        qs = slice(g * GROUP * HEAD_DIM, (g + 1) * GROUP * HEAD_DIM)
        qg = jnp.concatenate(
            [q_ref[:, qs.start + pp * 2 * HEAD_DIM:qs.start + (pp + 1) * 2 * HEAD_DIM] for pp in range(PAIRS)],
            axis=0)
        qg = (qg * (HEAD_DIM ** -0.5 * LOG2E)).astype(BF16)
        out = None
        for par in range(2):
            kx = jnp.concatenate([kcat, zeros] if par == 0 else [zeros, kcat], axis=1)
            vx = jnp.concatenate(([vcat, zeros] if par == 0 else [zeros, vcat]) + [ones], axis=1)
            s = _dot_nt(qg, kx) + bias_ref[which, g, par]
            p = jnp.exp2(s - jnp.max(s, axis=-1, keepdims=True))
            od = _dot(p.astype(BF16), vx)
            o = od[:, :2 * HEAD_DIM] / od[:, 2 * HEAD_DIM:]
            out = o if out is None else out + o
        for pp in range(PAIRS):
            o_ref[:, qs.start + pp * 2 * HEAD_DIM:qs.start + (pp + 1) * 2 * HEAD_DIM] = (
                out[pp * WINDOW:(pp + 1) * WINDOW].astype(o_ref.dtype))


def attn_prompt(qkv, bias_tab, sinks, seq_len):
    sink_rows = jnp.repeat(sinks.astype(F32).reshape(N_KV_HEADS, PAIRS, 2).transpose(0, 2, 1), WINDOW, axis=2)
    bias_tab = bias_tab.at[:, :, :, :, 0].set((sink_rows * LOG2E)[None])
    M = qkv.shape[0]
    nb = M // WINDOW
    bps = seq_len // WINDOW
    kcol = Q_DIM // KV_DIM
    prev = lambda r: jnp.maximum(r - 1, 0)
    return pl.pallas_call(
        functools.partial(_attn_prompt_kernel, blocks_per_seq=bps),
        grid=(nb,),
        in_specs=[
            pl.BlockSpec((WINDOW, Q_DIM), lambda r: (r, 0)),
            pl.BlockSpec((WINDOW, KV_DIM), lambda r: (r, kcol)),
            pl.BlockSpec((WINDOW, KV_DIM), lambda r: (prev(r), kcol)),
            pl.BlockSpec((WINDOW, KV_DIM), lambda r: (r, kcol + 1)),
            pl.BlockSpec((WINDOW, KV_DIM), lambda r: (prev(r), kcol + 1)),
            pl.BlockSpec((2, N_KV_HEADS, 2, PAIRS * WINDOW, 2 * WINDOW), lambda r: (0, 0, 0, 0, 0),
                         pipeline_mode=pl.Buffered(1)),
        ],
        out_specs=pl.BlockSpec((WINDOW, Q_DIM), lambda r: (r, 0)),
        out_shape=jax.ShapeDtypeStruct((M, Q_DIM), BF16),
        compiler_params=_cparams(("arbitrary",)),
        name="attn_prompt",
    )(qkv, qkv, qkv, qkv, qkv, bias_tab)


SAMPLE_SEQS_PER_STEP = 8


def _bdot(a, b, contract_a, contract_b):
    return lax.dot_general(a, b, (((contract_a,), (contract_b,)), ((0,), (0,))), preferred_element_type=F32)


def _attn_sample_kernel(q_ref, kn_ref, vn_ref, ck_ref, cv_ref, bias_ref, bias0_ref, sink_ref, o_ref):
    for g in range(N_KV_HEADS):
        ks = slice(g * HEAD_DIM, (g + 1) * HEAD_DIM)
        hs = slice(g * GROUP, (g + 1) * GROUP)
        qg = q_ref[:, hs, :] * (HEAD_DIM ** -0.5)
        kn = kn_ref[:, :, ks]
        vn = vn_ref[:, :, ks]
        s = _bdot(qg.astype(BF16), ck_ref[:, :, ks].astype(BF16), 2, 2) + bias_ref[hs, :][None]
        s_new = jnp.sum(qg * kn, axis=-1, keepdims=True) + bias0_ref[hs, :][None]
        sink = sink_ref[hs, :][None]
        m = jnp.maximum(jnp.maximum(jnp.max(s, axis=-1, keepdims=True), s_new), sink)
        p = jnp.exp(s - m)
        pn = jnp.exp(s_new - m)
        den = jnp.sum(p, axis=-1, keepdims=True) + pn + jnp.exp(sink - m)
        o = (_bdot(p.astype(BF16), cv_ref[:, :, ks].astype(BF16), 2, 1) + pn * vn) / den
        o_ref[:, hs, :] = o.astype(o_ref.dtype)


def attn_sample(q3, k_new, v_new, cache_k, cache_v, bias_s, bias0, sinks):
    nseq = cache_k.shape[0]
    sb = SAMPLE_SEQS_PER_STEP
    return pl.pallas_call(
        _attn_sample_kernel,
        grid=(nseq // sb,),
        in_specs=[
            pl.BlockSpec((sb, N_HEADS, HEAD_DIM), lambda b: (b, 0, 0)),
            pl.BlockSpec((sb, 1, KV_DIM), lambda b: (b, 0, 0)),
            pl.BlockSpec((sb, 1, KV_DIM), lambda b: (b, 0, 0)),
            pl.BlockSpec((sb, WINDOW, KV_DIM), lambda b: (b, 0, 0)),
            pl.BlockSpec((sb, WINDOW, KV_DIM), lambda b: (b, 0, 0)),
            pl.BlockSpec((N_HEADS, WINDOW), lambda b: (0, 0)),
            pl.BlockSpec((N_HEADS, 1), lambda b: (0, 0)),
            pl.BlockSpec((N_HEADS, 1), lambda b: (0, 0)),
        ],
        out_specs=pl.BlockSpec((sb, N_HEADS, HEAD_DIM), lambda b: (b, 0, 0)),
        out_shape=jax.ShapeDtypeStruct((nseq, N_HEADS, HEAD_DIM), F32),
        compiler_params=_cparams(("arbitrary",)),
        name="attn_sample",
    )(q3, k_new, v_new, cache_k, cache_v, bias_s, bias0, sinks)


CONV_HALO = 32
F32_SUBLANES = 8


def _conv_prompt_kernel(u_ref, halo_ref, w_ref, b_ref, o_ref, scr_ref, *, tt):
    first = pl.program_id(1) == 0

    @pl.when(first)
    def _():
        scr_ref[0:CONV_HALO, :] = jnp.zeros((CONV_HALO, scr_ref.shape[1]), F32)

    @pl.when(jnp.logical_not(first))
    def _():
        scr_ref[0:CONV_HALO, :] = halo_ref[0]

    scr_ref[CONV_HALO:, :] = u_ref[0]
    off = CONV_HALO - (CONV_WIDTH - 1)
    acc = jnp.broadcast_to(b_ref[...], (tt, scr_ref.shape[1]))
    for s in range(F32_SUBLANES):
        rows = tt if s == 0 else tt + F32_SUBLANES
        part = None
        for q in range(off, off + CONV_WIDTH):
            if q % F32_SUBLANES != s:
                continue
            term = scr_ref[q - s:q - s + rows, :] * w_ref[q - off:q - off + 1, :]
            part = term if part is None else part + term
        acc = acc + (part if s == 0 else part[s:s + tt])
    o_ref[0] = acc


def conv_prompt(u, w_dw, b_dw, layer):
    B, T, D = u.shape
    tt, dc = 512, 512
    hb = tt // CONV_HALO
    return pl.pallas_call(
        functools.partial(_conv_prompt_kernel, tt=tt),
        grid=(B, T // tt, D // dc),
        in_specs=[
            pl.BlockSpec((1, tt, dc), lambda b, i, j: (b, i, j)),
            pl.BlockSpec((1, CONV_HALO, dc), lambda b, i, j: (b, jnp.maximum(i * hb - 1, 0), j)),
            pl.BlockSpec((None, CONV_WIDTH, dc), lambda b, i, j: (layer, 0, j)),
            pl.BlockSpec((None, 1, dc), lambda b, i, j: (layer, 0, j)),
        ],
        out_specs=pl.BlockSpec((1, tt, dc), lambda b, i, j: (b, i, j)),
        out_shape=jax.ShapeDtypeStruct((B, T, D), F32),
        scratch_shapes=[pltpu.VMEM((CONV_HALO + tt, dc), F32)],
        compiler_params=_cparams(("arbitrary", "arbitrary", "arbitrary")),
        name="conv_prompt",
    )(u, u, w_dw, b_dw.reshape(b_dw.shape[0], 1, D))


def _conv_sample_kernel(st_ref, u_ref, w_ref, b_ref, o_ref):
    acc = u_ref[...] * w_ref[CONV_WIDTH - 1:CONV_WIDTH, :] + b_ref[...]
    for w in range(CONV_WIDTH - 1):
        acc = acc + st_ref[w] * w_ref[w:w + 1, :]
    o_ref[...] = acc


def conv_sample(state_t, u, w_dw, b_dw, layer):
    nseq, D = u.shape
    return pl.pallas_call(
        _conv_sample_kernel,
        grid=(1,),
        in_specs=[
            pl.BlockSpec((CONV_WIDTH - 1, nseq, D), lambda i: (0, 0, 0)),
            pl.BlockSpec((nseq, D), lambda i: (0, 0)),
            pl.BlockSpec((None, CONV_WIDTH, D), lambda i: (layer, 0, 0)),
            pl.BlockSpec((None, 1, D), lambda i: (layer, 0, 0)),
        ],
        out_specs=pl.BlockSpec((nseq, D), lambda i: (0, 0)),
        out_shape=jax.ShapeDtypeStruct((nseq, D), F32),
        compiler_params=_cparams(("arbitrary",)),
        name="conv_sample",
    )(state_t, u, w_dw, b_dw.reshape(b_dw.shape[0], 1, D))


def _ln_swish_kernel(y_ref, g_ref, b_ref, o_ref):
    y = y_ref[...]
    mu = jnp.mean(y, axis=-1, keepdims=True)
    yc = y - mu
    var = jnp.mean(yc * yc, axis=-1, keepdims=True)
    z = yc * lax.rsqrt(var + EPS) * g_ref[...] + b_ref[...]
    o_ref[...] = (z * jax.nn.sigmoid(z)).astype(o_ref.dtype)


def ln_swish(y, ln_g, ln_b, layer):
    M, D = y.shape
    tt = min(M, 512)
    return pl.pallas_call(
        _ln_swish_kernel,
        grid=(M // tt,),
        in_specs=[
            pl.BlockSpec((tt, D), lambda i: (i, 0)),
            pl.BlockSpec((None, 1, D), lambda i: (layer, 0, 0)),
            pl.BlockSpec((None, 1, D), lambda i: (layer, 0, 0)),
        ],
        out_specs=pl.BlockSpec((tt, D), lambda i: (i, 0)),
        out_shape=jax.ShapeDtypeStruct((M, D), BF16),
        compiler_params=_cparams(("arbitrary",)),
        name="ln_swish",
    )(y, ln_g.reshape(-1, 1, D), ln_b.reshape(-1, 1, D))


def _top_values(s, k):
    out = []
    for _ in range(k):
        m = jnp.max(s, axis=0, keepdims=True)
        out.append(m)
        s = jnp.where(s == m, -jnp.inf, s)
    return out


def _sorting_network(n):
    pairs = []

    def merge(lo, hi, r):
        step = 2 * r
        if step < hi - lo:
            merge(lo, hi, step)
            merge(lo + r, hi, step)
            pairs.extend((i, i + r) for i in range(lo + r, hi - r, step))
        else:
            pairs.append((lo, lo + r))

    def sort(lo, hi):
        if hi > lo:
            mid = lo + (hi - lo) // 2
            sort(lo, mid)
            sort(mid + 1, hi)
            merge(lo, hi, 1)

    sort(0, n - 1)
    return pairs


def _top_values_sorted(s, k, with_rank=False):
    groups = s.shape[0] // F32_SUBLANES
    lists = [s[r * F32_SUBLANES:(r + 1) * F32_SUBLANES] for r in range(groups)]
    for i, j in _sorting_network(groups):
        lists[i], lists[j] = jnp.maximum(lists[i], lists[j]), jnp.minimum(lists[i], lists[j])
    out = []
    for it in range(k):
        m = jnp.max(lists[0], axis=0, keepdims=True)
        out.append(m)
        live = min(groups, k - it) - 1
        if live <= 0:
            continue
        hit = lists[0] == m
        for d in range(live):
            lists[d] = jnp.where(hit, lists[d + 1], lists[d])
        if live == groups - 1:
            lists[live] = jnp.where(hit, -jnp.inf, lists[live])
    rank = None
    if with_rank:
        rank = jnp.full(s.shape, float(k), F32)
        for i in reversed(range(k)):
            rank = jnp.where(s >= out[i], float(i), rank)
    return out, rank


def _peer_topk_kernel(q_ref, keys_ref, n_ref, a_ref, r1_ref, e1_ref, top_ref):
    for h in range(PEER_HEADS):
        halves = []
        for p in range(2):
            hp = 2 * h + p
            qs = q_ref[:, hp * D_KEY_HALF:(hp + 1) * D_KEY_HALF].astype(BF16)
            s = _dot_nt(keys_ref[hp].astype(BF16), qs)
            tops, rank = _top_values_sorted(s, PEER_TOPK, with_rank=(p == 1))
            for k, m in enumerate(tops):
                top_ref[p, k:k + 1, :] = m
            halves.append((s, tops, rank))
        (s0, t0, _), (s1, t1, rank1) = halves
        t1_all = top_ref[1]
        pieces = ([t0[0] + t1_all] + [t0[a] + t1_all[0:8] for a in range(1, 8)]
                  + [top_ref[0, 8:PEER_TOPK, :] + t1[0]])
        best = _top_values(jnp.concatenate(pieces, axis=0), PEER_TOPK)
        tau = best[-1]
        z = jnp.exp(best[0] - best[0])
        for v in best[1:]:
            z = z + jnp.exp(v - best[0])
        n = jnp.zeros(s0.shape, F32)
        for a in range(PEER_TOPK):
            cnt = jnp.sum(jnp.where(t0[a] + t1_all >= tau, 1.0, 0.0), axis=0, keepdims=True)
            n = jnp.where(s0 == t0[a], cnt, n)
        n_ref[h] = n
        a_ref[h] = jnp.exp(s0 - t0[0])
        r1_ref[h] = rank1.astype(BF16)
        e1_ref[h] = (jnp.exp(s1 - t1[0]) / z).astype(BF16)


def peer_topk(q, sub_keys, layer):
    M = q.shape[0]
    tt = min(M, 256)
    keys = sub_keys.reshape(DEPTH, PEER_HEADS * 2, N_KEYS, D_KEY_HALF)
    return pl.pallas_call(
        _peer_topk_kernel,
        grid=(M // tt,),
        in_specs=[
            pl.BlockSpec((tt, PEER_HEADS * 2 * D_KEY_HALF), lambda i: (i, 0)),
            pl.BlockSpec((None, PEER_HEADS * 2, N_KEYS, D_KEY_HALF), lambda i: (layer, 0, 0, 0)),
        ],
        out_specs=[pl.BlockSpec((PEER_HEADS, N_KEYS, tt), lambda i: (0, 0, i))] * 4,
        out_shape=[
            jax.ShapeDtypeStruct((PEER_HEADS, N_KEYS, M), F32),
            jax.ShapeDtypeStruct((PEER_HEADS, N_KEYS, M), F32),
            jax.ShapeDtypeStruct((PEER_HEADS, N_KEYS, M), BF16),
            jax.ShapeDtypeStruct((PEER_HEADS, N_KEYS, M), BF16),
        ],
        scratch_shapes=[pltpu.VMEM((2, PEER_TOPK, tt), F32)],
        compiler_params=_cparams(("arbitrary",)),
        name="peer_topk",
    )(q, keys)


PEER_EC = 512


BF16_SUBLANES = 16


N_CHUNKS = N_EXPERTS // PEER_EC
GATE_TOKENS = 256


def _bcast_row_bf16(ref, h, i, ts):
    row = ref[h, pl.ds(i, 1), ts]
    rep = jnp.broadcast_to(row, (F32_SUBLANES, row.shape[1]))
    return jnp.concatenate([rep, rep], axis=0).astype(BF16)


def _peer_dense_kernel(hT_ref, n_ref, a_ref, r1_ref, e1_ref, u_ref, v_ref, o_ref,
                       act_ref, w_ref):
    e = pl.program_id(1)
    tt = hT_ref.shape[1]

    @pl.when(e == 0)
    def _():
        o_ref[...] = jnp.zeros(o_ref.shape, F32)

    act_ref[...] = _dot_mixed(u_ref[...], hT_ref[...])
    gate_tokens = min(GATE_TOKENS, tt)
    for tp in range(tt // gate_tokens):
        ts = slice(tp * gate_tokens, (tp + 1) * gate_tokens)
        for ii in range(PEER_EC // N_KEYS):
            i = e * (PEER_EC // N_KEYS) + ii
            g = None
            for h in range(PEER_HEADS):
                nrow = _bcast_row_bf16(n_ref, h, i, ts)
                arow = _bcast_row_bf16(a_ref, h, i, ts)
                gate = e1_ref[h, :, :, ts] * arow[None]
                val = jnp.where(r1_ref[h, :, :, ts] < nrow[None], gate, jnp.zeros_like(gate))
                g = val if g is None else g + val
            es = slice(ii * N_KEYS, (ii + 1) * N_KEYS)
            act = act_ref[es, ts]
            ge = (0.5 * act * (1.0 + lax.erf(act * (1.0 / math.sqrt(2.0))))).astype(BF16)
            w_ref[ts, es] = (ge * g.reshape(N_KEYS, gate_tokens)).T
    o_ref[...] += _dot_mixed(w_ref[...], v_ref[...])


def peer_dense(hT, n, a, r1, e1, u_tab, v_tab, layer):
    D, M = hT.shape
    tt = min(M, 1024)
    groups = N_KEYS // BF16_SUBLANES
    r1 = r1.reshape(PEER_HEADS, groups, BF16_SUBLANES, M)
    e1 = e1.reshape(PEER_HEADS, groups, BF16_SUBLANES, M)
    once = pl.Buffered(1)
    return pl.pallas_call(
        _peer_dense_kernel,
        grid=(M // tt, N_CHUNKS),
        in_specs=[
            pl.BlockSpec((D, tt), lambda t, e: (0, t), pipeline_mode=once),
            pl.BlockSpec((PEER_HEADS, N_KEYS, tt), lambda t, e: (0, 0, t), pipeline_mode=once),
            pl.BlockSpec((PEER_HEADS, N_KEYS, tt), lambda t, e: (0, 0, t), pipeline_mode=once),
            pl.BlockSpec((PEER_HEADS, groups, BF16_SUBLANES, tt), lambda t, e: (0, 0, 0, t), pipeline_mode=once),
            pl.BlockSpec((PEER_HEADS, groups, BF16_SUBLANES, tt), lambda t, e: (0, 0, 0, t), pipeline_mode=once),
            pl.BlockSpec((None, PEER_EC, D), lambda t, e: (layer, e, 0)),
            pl.BlockSpec((None, PEER_EC, D), lambda t, e: (layer, e, 0)),
        ],
        out_specs=pl.BlockSpec((tt, D), lambda t, e: (t, 0), pipeline_mode=once),
        out_shape=jax.ShapeDtypeStruct((M, D), F32),
        scratch_shapes=[pltpu.VMEM((PEER_EC, tt), F32), pltpu.VMEM((tt, PEER_EC), BF16)],
        compiler_params=_cparams(("arbitrary", "arbitrary")),
        name="peer_dense",
    )(hT, n, a, r1, e1, u_tab, v_tab)


def _t5_bucket(rel):
    n = jnp.maximum(rel, 0)
    max_exact = NUM_BUCKETS // 2
    nf = jnp.maximum(n, 1).astype(F32)
    large = max_exact + (jnp.log(nf / max_exact) / math.log(MAX_DISTANCE / max_exact)
                         * (NUM_BUCKETS - max_exact)).astype(jnp.int32)
    large = jnp.minimum(large, NUM_BUCKETS - 1)
    return jnp.where(n < max_exact, n, large)


def _bias_tables(rel_bias):
    rb = rel_bias.astype(F32)
    per_rel = rb[_t5_bucket(jnp.arange(WINDOW, dtype=jnp.int32))].T
    span = 3 * WINDOW - 1
    line = jnp.full((N_HEADS, span), NEG, F32).at[:, 1:WINDOW + 1].set(per_rel[:, ::-1])
    skew = jnp.tile(line, (1, WINDOW))[:, :WINDOW * (span - 1)].reshape(N_HEADS, WINDOW, span - 1)
    tab = skew[:, :, :2 * WINDOW]
    bias_p = (tab.reshape(N_KV_HEADS, PAIRS, 2, WINDOW, 2 * WINDOW)
              .transpose(0, 2, 1, 3, 4).reshape(N_KV_HEADS, 2, PAIRS * WINDOW, 2 * WINDOW))
    bias_p = bias_p * LOG2E
    no_prev = jnp.arange(2 * WINDOW) < WINDOW
    bias_p = jnp.stack([bias_p, jnp.where(no_prev, NEG * LOG2E, bias_p)])
    rel_s = WINDOW - jnp.arange(WINDOW, dtype=jnp.int32)
    tab_s = jnp.where((rel_s < WINDOW)[:, None], rb[_t5_bucket(rel_s)], NEG)
    bias_s = tab_s.T
    bias0 = rb[_t5_bucket(jnp.zeros((1,), jnp.int32))].T
    return bias_p, bias_s, bias0


def _peer_layer(x, g_ffn_i, sc, sh, gt, w_pq, sub_keys, u_tab, v_tab, i):
    G, R, D = x.shape
    h, hT = norm_mod(x, g_ffn_i, sc, sh, transposed=True)
    q = matmul(h.reshape(G * R, D), w_pq, i)
    n, a, r1, e1 = peer_topk(q, sub_keys, i)
    y = peer_dense(hT, n, a, r1, e1, u_tab, v_tab, i)
    return residual(x, y.reshape(G, R, D), gt)


def _split_mod(mod):
    return [mod[:, :, k * D_MODEL:(k + 1) * D_MODEL] for k in range(6)]


def kernel(x_prompt, x_sample, cache_k, cache_v, state_conv, c_prompt, c_sample, rel_bias,
           w_ada, b_ada, g_mix, g_ffn, g_final, w_qkv, b_qkv, w_o, sinks,
           w_pw1, b_pw1, w_dw, b_dw, ln_g, ln_b, w_pw2, b_pw2, w_pq, sub_keys, u_tab, v_tab):
    B, T, D = x_prompt.shape
    S = x_sample.shape[0]
    n_attn = w_qkv.shape[0]

    rows = B + S
    rows_pad = -(-rows // 16) * 16
    c_all = jnp.concatenate([c_prompt, c_sample, jnp.zeros((rows_pad - rows, D), F32)], axis=0)
    mod = ada_mod(c_all, w_ada, b_ada)
    bias_p, bias_s, bias0 = _bias_tables(rel_bias)

    xp = x_prompt
    xs = jnp.pad(x_sample.reshape(1, S, D), ((0, 0), (0, SAMPLE_PAD - S), (0, 0)))
    new_kp, new_vp, new_cp, new_ks, new_vs, new_cs = [], [], [], [], [], []
    tm_rows = T

    for i in range(DEPTH):
        mp = _split_mod(mod[i, :B].reshape(B, 1, 6 * D))
        ms = _split_mod(jnp.pad(mod[i, B:B + S], ((0, SAMPLE_PAD - S), (0, 0))).reshape(1, SAMPLE_PAD, 6 * D))
        hp = norm_mod(xp, g_mix[i], mp[1], mp[0]).reshape(B * T, D)
        hs = norm_mod(xs, g_mix[i], ms[1], ms[0]).reshape(SAMPLE_PAD, D)
        if i % 2 == 0:
            a = i // 2
            sink_a = sinks[a].astype(F32)
            qkv = matmul(hp, w_qkv, a, b_qkv)
            o = attn_prompt(qkv, bias_p, sink_a, T)
            xp = matmul(o, w_o, a, mode="resid", xres=xp.reshape(B * T, D), gate=mp[2],
                        rows_per_gate=tm_rows).reshape(B, T, D)
            kv = qkv.reshape(B, T, Q_DIM + 2 * KV_DIM)[:, T - WINDOW:, Q_DIM:]
            new_kp.append(kv[..., :KV_DIM].reshape(B, WINDOW, N_KV_HEADS, HEAD_DIM))
            new_vp.append(kv[..., KV_DIM:].reshape(B, WINDOW, N_KV_HEADS, HEAD_DIM))
            qkv_s = matmul(hs, w_qkv, a, b_qkv)
            q3 = qkv_s[:S, :Q_DIM].reshape(S, N_HEADS, HEAD_DIM)
            k_new = qkv_s[:S, Q_DIM:Q_DIM + KV_DIM]
            v_new = qkv_s[:S, Q_DIM + KV_DIM:]
            ck = cache_k[a].reshape(S, WINDOW, KV_DIM)
            cv = cache_v[a].reshape(S, WINDOW, KV_DIM)
            o_s = attn_sample(q3, k_new.reshape(S, 1, KV_DIM), v_new.reshape(S, 1, KV_DIM), ck, cv,
                              bias_s, bias0, sink_a.reshape(N_HEADS, 1))
            o_s = jnp.pad(o_s.reshape(S, Q_DIM), ((0, SAMPLE_PAD - S), (0, 0))).astype(BF16)
            xs = matmul(o_s, w_o, a, mode="resid",
                        xres=xs.reshape(SAMPLE_PAD, D), gate=ms[2]).reshape(1, SAMPLE_PAD, D)
            new_ks.append(jnp.concatenate([ck[:, 1:], k_new[:, None]], axis=1)
                          .reshape(S, WINDOW, N_KV_HEADS, HEAD_DIM))
            new_vs.append(jnp.concatenate([cv[:, 1:], v_new[:, None]], axis=1)
                          .reshape(S, WINDOW, N_KV_HEADS, HEAD_DIM))
        else:
            bl = i // 2
            u = matmul(hp, w_pw1, bl, b_pw1, mode="glu", n_out=D).reshape(B, T, D)
            y = conv_prompt(u, w_dw, b_dw, bl)
            z = ln_swish(y.reshape(B * T, D), ln_g, ln_b, bl)
            xp = matmul(z, w_pw2, bl, b_pw2, mode="resid", xres=xp.reshape(B * T, D), gate=mp[2],
                        rows_per_gate=tm_rows).reshape(B, T, D)
            new_cp.append(u[:, T - (CONV_WIDTH - 1):])
            u_s = matmul(hs, w_pw1, bl, b_pw1, mode="glu", n_out=D)
            st = state_conv[bl]
            y_s = conv_sample(st.transpose(1, 0, 2), u_s[:S], w_dw, b_dw, bl)
            z_s = ln_swish(jnp.pad(y_s, ((0, SAMPLE_PAD - S), (0, 0))), ln_g, ln_b, bl)
            xs = matmul(z_s, w_pw2, bl, b_pw2, mode="resid", xres=xs.reshape(SAMPLE_PAD, D),
                        gate=ms[2]).reshape(1, SAMPLE_PAD, D)
            new_cs.append(jnp.concatenate([st[:, 1:], u_s[:S, None]], axis=1))
        xp = _peer_layer(xp, g_ffn[i], mp[4], mp[3], mp[5], w_pq, sub_keys, u_tab, v_tab, i)
        xs = _peer_layer(xs, g_ffn[i], ms[4], ms[3], ms[5], w_pq, sub_keys, u_tab, v_tab, i)

    y_prompt = norm_mod(xp, g_final, out_dtype=F32)
    y_sample = norm_mod(xs, g_final, out_dtype=F32)[0, :S].reshape(S, 1, D)
    return (y_prompt, y_sample, jnp.stack(new_kp), jnp.stack(new_vp), jnp.stack(new_cp),
            jnp.stack(new_ks), jnp.stack(new_vs), jnp.stack(new_cs))
```

```python
import functools
import math

import jax
import jax.numpy as jnp
import numpy as np
from jax import lax
from jax.experimental import pallas as pl
from jax.experimental.pallas import tpu as pltpu

D_MODEL = 2048
DEPTH = 4
N_HEADS = 64
N_KV_HEADS = 8
HEAD_DIM = 64
GROUP = N_HEADS // N_KV_HEADS
Q_DIM = N_HEADS * HEAD_DIM
KV_DIM = N_KV_HEADS * HEAD_DIM
WINDOW = 128
NUM_BUCKETS = 32
MAX_DISTANCE = 128
CONV_WIDTH = 31
PEER_HEADS = 8
N_KEYS = 128
N_EXPERTS = N_KEYS * N_KEYS
PEER_TOPK = 16
D_KEY_HALF = 128
EPS = 1e-6
NEG = -1e30
LOG2E = 1.4426950408889634

VMEM_LIMIT_V7X = 56 * 1024 * 1024
SAMPLE_PAD = 128

BF16 = jnp.bfloat16
F32 = jnp.float32


def _cparams(sem, flags=None):
    return pltpu.CompilerParams(dimension_semantics=sem, vmem_limit_bytes=VMEM_LIMIT_V7X, flags=flags)


def _dot(a, b):
    return jnp.dot(a, b, preferred_element_type=F32)


def _dot_mixed(a, b):
    return lax.dot_general(a, b, (((1,), (0,)), ((), ())), preferred_element_type=F32)


def _dot_nt(a, b):
    return lax.dot_general(a, b, (((1,), (1,)), ((), ())), preferred_element_type=F32)


def _dot_tn(a, b):
    return lax.dot_general(a, b, (((0,), (0,)), ((), ())), preferred_element_type=F32)


def _ada_kernel(c_ref, w_ref, b_ref, o_ref):
    c = c_ref[...]
    cond = (c * jax.nn.sigmoid(c)).astype(BF16)
    o_ref[...] = _dot(cond, w_ref[...].astype(BF16)) + b_ref[...]


def ada_mod(c_all, w_ada, b_ada):
    rows = c_all.shape[0]
    n = w_ada.shape[2]
    tn = 1536
    return pl.pallas_call(
        _ada_kernel,
        grid=(DEPTH, n // tn),
        in_specs=[
            pl.BlockSpec((rows, D_MODEL), lambda l, j: (0, 0)),
            pl.BlockSpec((None, D_MODEL, tn), lambda l, j: (l, 0, j)),
            pl.BlockSpec((None, 1, tn), lambda l, j: (l, 0, j)),
        ],
        out_specs=pl.BlockSpec((None, rows, tn), lambda l, j: (l, 0, j)),
        out_shape=jax.ShapeDtypeStruct((DEPTH, rows, n), F32),
        compiler_params=_cparams(("arbitrary", "arbitrary")),
        name="ada_mod",
    )(c_all, w_ada, b_ada.reshape(DEPTH, 1, n))


def _norm_kernel(*refs, modulated, transposed, residual, keep_x):
    it = iter(refs)
    x_ref = next(it)
    if residual:
        y_ref, gate_ref = next(it), next(it)
    g_ref = next(it)
    if modulated:
        sc_ref, sh_ref = next(it), next(it)
    outs = list(it)
    x = x_ref[0]
    if residual:
        x = x + gate_ref[0] * y_ref[0]
        if keep_x:
            outs.pop(0)[0] = x
    y = x * lax.rsqrt(jnp.mean(x * x, axis=-1, keepdims=True) + EPS) * g_ref[...]
    if modulated:
        y = y * (1.0 + sc_ref[0]) + sh_ref[0]
    outs[0][0] = y.astype(outs[0].dtype)
    if transposed:
        outs[1][...] = y.T.astype(outs[1].dtype)


def norm_mod(x, g, sc=None, sh=None, *, out_dtype=BF16, transposed=False, resid=None, keep_x=True):
    G, R, D = x.shape
    tt = min(R, 512)
    nt = R // tt
    modulated = sc is not None
    residual = resid is not None
    row_spec = pl.BlockSpec((1, tt, D), lambda b, i: (b, i, 0))
    seq_spec = pl.BlockSpec((1, 1, D), lambda b, i: (b, 0, 0))
    per_rows = lambda arr: seq_spec if arr.shape[1] == 1 else row_spec
    in_specs = [row_spec]
    args = [x]
    if residual:
        y, gate = resid
        in_specs += [row_spec, per_rows(gate)]
        args += [y, gate]
    in_specs.append(pl.BlockSpec((1, D), lambda b, i: (0, 0)))
    args.append(g.reshape(1, D))
    if modulated:
        in_specs += [per_rows(sc), per_rows(sh)]
        args += [sc, sh]
    out_specs, out_shape = [], []
    if residual and keep_x:
        out_specs.append(row_spec)
        out_shape.append(jax.ShapeDtypeStruct((G, R, D), F32))
    out_specs.append(row_spec)
    out_shape.append(jax.ShapeDtypeStruct((G, R, D), out_dtype))
    if transposed:
        out_specs.append(pl.BlockSpec((D, tt), lambda b, i: (0, b * nt + i)))
        out_shape.append(jax.ShapeDtypeStruct((D, G * R), BF16))
    res = pl.pallas_call(
        functools.partial(_norm_kernel, modulated=modulated, transposed=transposed,
                          residual=residual, keep_x=keep_x),
        grid=(G, nt),
        in_specs=in_specs,
        out_specs=out_specs,
        out_shape=out_shape,
        compiler_params=_cparams(("arbitrary", "arbitrary")),
        name="norm_mod",
    )(*args)
    return res if len(res) > 1 else res[0]


def _mm_kernel(*refs, mode, has_bias):
    it = iter(refs)
    h_ref = next(it)
    w_ref = next(it)
    w2_ref = next(it) if mode == "glu" else None
    b_ref = next(it) if has_bias else None
    b2_ref = next(it) if (mode == "glu" and has_bias) else None
    if mode == "resid":
        x_ref = next(it)
        gate_ref = next(it)
    o_ref = next(it)
    ws_ref = next(it)
    ws2_ref = next(it) if mode == "glu" else None

    @pl.when(pl.program_id(1) == 0)
    def _():
        ws_ref[...] = w_ref[...].astype(BF16)
        if mode == "glu":
            ws2_ref[...] = w2_ref[...].astype(BF16)

    h = h_ref[...]
    acc = _dot(h, ws_ref[...])
    if has_bias:
        acc = acc + b_ref[...]
    if mode == "glu":
        gte = _dot(h, ws2_ref[...])
        if has_bias:
            gte = gte + b2_ref[...]
        acc = acc * jax.nn.sigmoid(gte)
    if mode == "resid":
        acc = x_ref[...] + gate_ref[0] * acc
    o_ref[...] = acc.astype(o_ref.dtype)


PROJ_VMEM_BUDGET = 48 * 1024 * 1024


def _proj_col_tile(tm, k, n, *, n_weights, out_bytes, resid):
    for tn in (1024, 512, 256, 128):
        if n % tn:
            continue
        need = (2 * tm * k * 2 + n_weights * (2 * k * tn * 4 + k * tn * 2)
                + 2 * tm * tn * out_bytes + (2 * tm * tn * 4 if resid else 0))
        if need <= PROJ_VMEM_BUDGET:
            return tn
    raise ValueError("no projection column tile fits VMEM")


def matmul(h, w, layer, bias=None, *, mode="plain", n_out=None, xres=None, gate=None,
           rows_per_gate=None, out_dtype=F32):
    M, K = h.shape
    nw = w.shape[2]
    N = n_out if n_out is not None else nw
    tm = min(M, 1024)
    tn = _proj_col_tile(tm, K, N, n_weights=2 if mode == "glu" else 1,
                        out_bytes=jnp.dtype(out_dtype).itemsize, resid=(mode == "resid"))
    nj, ni = N // tn, M // tm
    has_bias = bias is not None
    in_specs = [
        pl.BlockSpec((tm, K), lambda j, i: (i, 0)),
        pl.BlockSpec((None, K, tn), lambda j, i: (layer, 0, j)),
    ]
    args = [h, w]
    if mode == "glu":
        in_specs.append(pl.BlockSpec((None, K, tn), lambda j, i: (layer, 0, j + nj)))
        args.append(w)
    if has_bias:
        b3 = bias.reshape(bias.shape[0], 1, nw)
        in_specs.append(pl.BlockSpec((None, 1, tn), lambda j, i: (layer, 0, j)))
        args.append(b3)
        if mode == "glu":
            in_specs.append(pl.BlockSpec((None, 1, tn), lambda j, i: (layer, 0, j + nj)))
            args.append(b3)
    if mode == "resid":
        in_specs.append(pl.BlockSpec((tm, tn), lambda j, i: (i, j)))
        args.append(xres)
        if gate.shape[1] == 1:
            tiles_per_gate = rows_per_gate // tm
            in_specs.append(pl.BlockSpec((1, 1, tn), lambda j, i: (i // tiles_per_gate, 0, j)))
        else:
            in_specs.append(pl.BlockSpec((1, tm, tn), lambda j, i: (0, i, j)))
        args.append(gate)
    scratch = [pltpu.VMEM((K, tn), BF16)]
    if mode == "glu":
        scratch.append(pltpu.VMEM((K, tn), BF16))
    return pl.pallas_call(
        functools.partial(_mm_kernel, mode=mode, has_bias=has_bias),
        grid=(nj, ni),
        in_specs=in_specs,
        out_specs=pl.BlockSpec((tm, tn), lambda j, i: (i, j)),
        out_shape=jax.ShapeDtypeStruct((M, N), out_dtype),
        scratch_shapes=scratch,
        compiler_params=_cparams(("arbitrary", "arbitrary")),
        name="proj_" + mode,
    )(*args)


PAIRS = GROUP // 2


def _attn_prompt_kernel(q_ref, kc_ref, kp_ref, vc_ref, vp_ref, bias_ref, o_ref,
                        *, blocks_per_seq):
    which = ((pl.program_id(0) % blocks_per_seq) == 0).astype(jnp.int32)
    zeros = jnp.zeros((2 * WINDOW, HEAD_DIM), BF16)
    ones = jnp.ones((2 * WINDOW, 2 * HEAD_DIM), BF16)
    not_sink = lax.broadcasted_iota(jnp.int32, (2 * WINDOW, HEAD_DIM), 0) > 0
    for g in range(N_KV_HEADS):
        ks = slice(g * HEAD_DIM, (g + 1) * HEAD_DIM)
        kcat = jnp.concatenate([kp_ref[:, ks], kc_ref[:, ks]], axis=0)
        vcat = jnp.concatenate([vp_ref[:, ks], vc_ref[:, ks]], axis=0)
        kcat = jnp.where(not_sink, kcat, 0.0).astype(BF16)
        vcat = jnp.where(not_sink, vcat, 0.0).astype(BF16)
        qs = slice(g * GROUP * HEAD_DIM, (g + 1) * GROUP * HEAD_DIM)
        qg = jnp.concatenate(
            [q_ref[:, qs.start + pp * 2 * HEAD_DIM:qs.start + (pp + 1) * 2 * HEAD_DIM] for pp in range(PAIRS)],
            axis=0)
        qg = (qg * (HEAD_DIM ** -0.5 * LOG2E)).astype(BF16)
        out = None
        for par in range(2):
            kx = jnp.concatenate([kcat, zeros] if par == 0 else [zeros, kcat], axis=1)
            vx = jnp.concatenate(([vcat, zeros] if par == 0 else [zeros, vcat]) + [ones], axis=1)
            s = _dot_nt(qg, kx) + bias_ref[which, g, par]
            p = jnp.exp2(s - jnp.max(s, axis=-1, keepdims=True))
            od = _dot(p.astype(BF16), vx)
            o = od[:, :2 * HEAD_DIM] / od[:, 2 * HEAD_DIM:]
            out = o if out is None else out + o
        for pp in range(PAIRS):
            o_ref[:, qs.start + pp * 2 * HEAD_DIM:qs.start + (pp + 1) * 2 * HEAD_DIM] = (
                out[pp * WINDOW:(pp + 1) * WINDOW].astype(o_ref.dtype))


def attn_prompt(qkv, bias_tab, sinks, seq_len):
    sink_rows = jnp.repeat(sinks.astype(F32).reshape(N_KV_HEADS, PAIRS, 2).transpose(0, 2, 1), WINDOW, axis=2)
    bias_tab = bias_tab.at[:, :, :, :, 0].set((sink_rows * LOG2E)[None])
    M = qkv.shape[0]
    nb = M // WINDOW
    bps = seq_len // WINDOW
    kcol = Q_DIM // KV_DIM
    prev = lambda r: jnp.maximum(r - 1, 0)
    return pl.pallas_call(
        functools.partial(_attn_prompt_kernel, blocks_per_seq=bps),
        grid=(nb,),
        in_specs=[
            pl.BlockSpec((WINDOW, Q_DIM), lambda r: (r, 0)),
            pl.BlockSpec((WINDOW, KV_DIM), lambda r: (r, kcol)),
            pl.BlockSpec((WINDOW, KV_DIM), lambda r: (prev(r), kcol)),
            pl.BlockSpec((WINDOW, KV_DIM), lambda r: (r, kcol + 1)),
            pl.BlockSpec((WINDOW, KV_DIM), lambda r: (prev(r), kcol + 1)),
            pl.BlockSpec((2, N_KV_HEADS, 2, PAIRS * WINDOW, 2 * WINDOW), lambda r: (0, 0, 0, 0, 0),
                         pipeline_mode=pl.Buffered(1)),
        ],
        out_specs=pl.BlockSpec((WINDOW, Q_DIM), lambda r: (r, 0)),
        out_shape=jax.ShapeDtypeStruct((M, Q_DIM), BF16),
        compiler_params=_cparams(("arbitrary",)),
        name="attn_prompt",
    )(qkv, qkv, qkv, qkv, qkv, bias_tab)


SAMPLE_SEQS_PER_STEP = 8


def _bdot(a, b, contract_a, contract_b):
    return lax.dot_general(a, b, (((contract_a,), (contract_b,)), ((0,), (0,))), preferred_element_type=F32)


def _attn_sample_kernel(q_ref, kn_ref, vn_ref, ck_ref, cv_ref, bias_ref, bias0_ref, sink_ref, o_ref):
    for g in range(N_KV_HEADS):
        ks = slice(g * HEAD_DIM, (g + 1) * HEAD_DIM)
        hs = slice(g * GROUP, (g + 1) * GROUP)
        qg = q_ref[:, hs, :] * (HEAD_DIM ** -0.5)
        kn = kn_ref[:, :, ks]
        vn = vn_ref[:, :, ks]
        s = _bdot(qg.astype(BF16), ck_ref[:, :, ks].astype(BF16), 2, 2) + bias_ref[hs, :][None]
        s_new = jnp.sum(qg * kn, axis=-1, keepdims=True) + bias0_ref[hs, :][None]
        sink = sink_ref[hs, :][None]
        m = jnp.maximum(jnp.maximum(jnp.max(s, axis=-1, keepdims=True), s_new), sink)
        p = jnp.exp(s - m)
        pn = jnp.exp(s_new - m)
        den = jnp.sum(p, axis=-1, keepdims=True) + pn + jnp.exp(sink - m)
        o = (_bdot(p.astype(BF16), cv_ref[:, :, ks].astype(BF16), 2, 1) + pn * vn) / den
        o_ref[:, hs, :] = o.astype(o_ref.dtype)


def attn_sample(q3, k_new, v_new, cache_k, cache_v, bias_s, bias0, sinks):
    nseq = cache_k.shape[0]
    sb = SAMPLE_SEQS_PER_STEP
    return pl.pallas_call(
        _attn_sample_kernel,
        grid=(nseq // sb,),
        in_specs=[
            pl.BlockSpec((sb, N_HEADS, HEAD_DIM), lambda b: (b, 0, 0)),
            pl.BlockSpec((sb, 1, KV_DIM), lambda b: (b, 0, 0)),
            pl.BlockSpec((sb, 1, KV_DIM), lambda b: (b, 0, 0)),
            pl.BlockSpec((sb, WINDOW, KV_DIM), lambda b: (b, 0, 0)),
            pl.BlockSpec((sb, WINDOW, KV_DIM), lambda b: (b, 0, 0)),
            pl.BlockSpec((N_HEADS, WINDOW), lambda b: (0, 0)),
            pl.BlockSpec((N_HEADS, 1), lambda b: (0, 0)),
            pl.BlockSpec((N_HEADS, 1), lambda b: (0, 0)),
        ],
        out_specs=pl.BlockSpec((sb, N_HEADS, HEAD_DIM), lambda b: (b, 0, 0)),
        out_shape=jax.ShapeDtypeStruct((nseq, N_HEADS, HEAD_DIM), F32),
        compiler_params=_cparams(("arbitrary",)),
        name="attn_sample",
    )(q3, k_new, v_new, cache_k, cache_v, bias_s, bias0, sinks)


CONV_HALO = 32
F32_SUBLANES = 8


def _conv_prompt_kernel(u_ref, halo_ref, w_ref, b_ref, o_ref, scr_ref, *, tt):
    first = pl.program_id(1) == 0

    @pl.when(first)
    def _():
        scr_ref[0:CONV_HALO, :] = jnp.zeros((CONV_HALO, scr_ref.shape[1]), F32)

    @pl.when(jnp.logical_not(first))
    def _():
        scr_ref[0:CONV_HALO, :] = halo_ref[0]

    scr_ref[CONV_HALO:, :] = u_ref[0]
    off = CONV_HALO - (CONV_WIDTH - 1)
    acc = jnp.broadcast_to(b_ref[...], (tt, scr_ref.shape[1]))
    for s in range(F32_SUBLANES):
        rows = tt if s == 0 else tt + F32_SUBLANES
        part = None
        for q in range(off, off + CONV_WIDTH):
            if q % F32_SUBLANES != s:
                continue
            term = scr_ref[q - s:q - s + rows, :] * w_ref[q - off:q - off + 1, :]
            part = term if part is None else part + term
        acc = acc + (part if s == 0 else part[s:s + tt])
    o_ref[0] = acc


def conv_prompt(u, w_dw, b_dw, layer):
    B, T, D = u.shape
    tt, dc = 512, 512
    hb = tt // CONV_HALO
    return pl.pallas_call(
        functools.partial(_conv_prompt_kernel, tt=tt),
        grid=(B, T // tt, D // dc),
        in_specs=[
            pl.BlockSpec((1, tt, dc), lambda b, i, j: (b, i, j)),
            pl.BlockSpec((1, CONV_HALO, dc), lambda b, i, j: (b, jnp.maximum(i * hb - 1, 0), j)),
            pl.BlockSpec((None, CONV_WIDTH, dc), lambda b, i, j: (layer, 0, j)),
            pl.BlockSpec((None, 1, dc), lambda b, i, j: (layer, 0, j)),
        ],
        out_specs=pl.BlockSpec((1, tt, dc), lambda b, i, j: (b, i, j)),
        out_shape=jax.ShapeDtypeStruct((B, T, D), F32),
        scratch_shapes=[pltpu.VMEM((CONV_HALO + tt, dc), F32)],
        compiler_params=_cparams(("arbitrary", "arbitrary", "arbitrary")),
        name="conv_prompt",
    )(u, u, w_dw, b_dw.reshape(b_dw.shape[0], 1, D))


def _conv_sample_kernel(st_ref, u_ref, w_ref, b_ref, o_ref):
    acc = u_ref[...] * w_ref[CONV_WIDTH - 1:CONV_WIDTH, :] + b_ref[...]
    for w in range(CONV_WIDTH - 1):
        acc = acc + st_ref[w] * w_ref[w:w + 1, :]
    o_ref[...] = acc


def conv_sample(state_t, u, w_dw, b_dw, layer):
    nseq, D = u.shape
    return pl.pallas_call(
        _conv_sample_kernel,
        grid=(1,),
        in_specs=[
            pl.BlockSpec((CONV_WIDTH - 1, nseq, D), lambda i: (0, 0, 0)),
            pl.BlockSpec((nseq, D), lambda i: (0, 0)),
            pl.BlockSpec((None, CONV_WIDTH, D), lambda i: (layer, 0, 0)),
            pl.BlockSpec((None, 1, D), lambda i: (layer, 0, 0)),
        ],
        out_specs=pl.BlockSpec((nseq, D), lambda i: (0, 0)),
        out_shape=jax.ShapeDtypeStruct((nseq, D), F32),
        compiler_params=_cparams(("arbitrary",)),
        name="conv_sample",
    )(state_t, u, w_dw, b_dw.reshape(b_dw.shape[0], 1, D))


def _ln_swish_kernel(y_ref, g_ref, b_ref, o_ref):
    y = y_ref[...]
    mu = jnp.mean(y, axis=-1, keepdims=True)
    yc = y - mu
    var = jnp.mean(yc * yc, axis=-1, keepdims=True)
    z = yc * lax.rsqrt(var + EPS) * g_ref[...] + b_ref[...]
    o_ref[...] = (z * jax.nn.sigmoid(z)).astype(o_ref.dtype)


def ln_swish(y, ln_g, ln_b, layer):
    M, D = y.shape
    tt = min(M, 512)
    return pl.pallas_call(
        _ln_swish_kernel,
        grid=(M // tt,),
        in_specs=[
            pl.BlockSpec((tt, D), lambda i: (i, 0)),
            pl.BlockSpec((None, 1, D), lambda i: (layer, 0, 0)),
            pl.BlockSpec((None, 1, D), lambda i: (layer, 0, 0)),
        ],
        out_specs=pl.BlockSpec((tt, D), lambda i: (i, 0)),
        out_shape=jax.ShapeDtypeStruct((M, D), BF16),
        compiler_params=_cparams(("arbitrary",)),
        name="ln_swish",
    )(y, ln_g.reshape(-1, 1, D), ln_b.reshape(-1, 1, D))


def _sorting_network(n):
    pairs = []

    def merge(lo, hi, r):
        step = 2 * r
        if step < hi - lo:
            merge(lo, hi, step)
            merge(lo + r, hi, step)
            pairs.extend((i, i + r) for i in range(lo + r, hi - r, step))
        else:
            pairs.append((lo, lo + r))

    def sort(lo, hi):
        if hi > lo:
            mid = lo + (hi - lo) // 2
            sort(lo, mid)
            sort(mid + 1, hi)
            merge(lo, hi, 1)

    sort(0, n - 1)
    return pairs


def _top_values_sorted(s, k, with_rank=False):
    groups = s.shape[0] // F32_SUBLANES
    lists = [s[r * F32_SUBLANES:(r + 1) * F32_SUBLANES] for r in range(groups)]
    for i, j in _sorting_network(pl.next_power_of_2(groups)):
        if j < groups:
            lists[i], lists[j] = jnp.maximum(lists[i], lists[j]), jnp.minimum(lists[i], lists[j])
    out = []
    for it in range(k):
        m = jnp.max(lists[0], axis=0, keepdims=True)
        out.append(m)
        live = min(groups, k - it) - 1
        if live <= 0:
            continue
        hit = lists[0] == m
        for d in range(live):
            lists[d] = jnp.where(hit, lists[d + 1], lists[d])
        if live == groups - 1:
            lists[live] = jnp.where(hit, -jnp.inf, lists[live])
    rank = None
    if with_rank:
        rank = jnp.full(s.shape, float(k), F32)
        for i in reversed(range(k)):
            rank = jnp.where(s >= out[i], float(i), rank)
    return out, rank


def _peer_topk_kernel(q_ref, keys_ref, n_ref, a_ref, r1_ref, e1_ref, top_ref):
    for h in range(PEER_HEADS):
        halves = []
        for p in range(2):
            hp = 2 * h + p
            qs = q_ref[:, hp * D_KEY_HALF:(hp + 1) * D_KEY_HALF].astype(BF16)
            s = _dot_nt(keys_ref[hp].astype(BF16), qs)
            tops, rank = _top_values_sorted(s, PEER_TOPK, with_rank=(p == 1))
            for k, m in enumerate(tops):
                top_ref[p, k:k + 1, :] = m
            halves.append((s, tops, rank))
        (s0, t0, _), (s1, t1, rank1) = halves
        t1_all = top_ref[1]
        pieces = ([t0[0] + t1_all] + [t0[a] + t1_all[0:8] for a in range(1, 8)]
                  + [top_ref[0, 8:PEER_TOPK, :] + t1[0]])
        best, _ = _top_values_sorted(jnp.concatenate(pieces, axis=0), PEER_TOPK)
        tau = best[-1]
        z = jnp.exp(best[0] - best[0])
        for v in best[1:]:
            z = z + jnp.exp(v - best[0])
        n = jnp.zeros(s0.shape, F32)
        for a in range(PEER_TOPK):
            cnt = jnp.sum(jnp.where(t0[a] + t1_all >= tau, 1.0, 0.0), axis=0, keepdims=True)
            n = jnp.where(s0 == t0[a], cnt, n)
        n_ref[h] = n
        a_ref[h] = jnp.exp(s0 - t0[0])
        r1_ref[h] = rank1.astype(BF16)
        e1_ref[h] = (jnp.exp(s1 - t1[0]) / z).astype(BF16)


def peer_topk(q, sub_keys, layer):
    M = q.shape[0]
    tt = min(M, 256)
    keys = sub_keys.reshape(DEPTH, PEER_HEADS * 2, N_KEYS, D_KEY_HALF)
    return pl.pallas_call(
        _peer_topk_kernel,
        grid=(M // tt,),
        in_specs=[
            pl.BlockSpec((tt, PEER_HEADS * 2 * D_KEY_HALF), lambda i: (i, 0)),
            pl.BlockSpec((None, PEER_HEADS * 2, N_KEYS, D_KEY_HALF), lambda i: (layer, 0, 0, 0)),
        ],
        out_specs=[pl.BlockSpec((PEER_HEADS, N_KEYS, tt), lambda i: (0, 0, i))] * 4,
        out_shape=[
            jax.ShapeDtypeStruct((PEER_HEADS, N_KEYS, M), F32),
            jax.ShapeDtypeStruct((PEER_HEADS, N_KEYS, M), F32),
            jax.ShapeDtypeStruct((PEER_HEADS, N_KEYS, M), BF16),
            jax.ShapeDtypeStruct((PEER_HEADS, N_KEYS, M), BF16),
        ],
        scratch_shapes=[pltpu.VMEM((2, PEER_TOPK, tt), F32)],
        compiler_params=_cparams(("arbitrary",)),
        name="peer_topk",
    )(q, keys)


PEER_EC = 512


BF16_SUBLANES = 16


N_CHUNKS = N_EXPERTS // PEER_EC
GATE_TOKENS = 256


def _bcast_row_bf16(ref, h, i, ts):
    row = ref[h, pl.ds(i, 1), ts]
    rep = jnp.broadcast_to(row, (F32_SUBLANES, row.shape[1]))
    return jnp.concatenate([rep, rep], axis=0).astype(BF16)


def _peer_dense_kernel(hT_ref, n_ref, a_ref, r1_ref, e1_ref, u_ref, v_ref, o_ref,
                       act_ref, w_ref):
    e = pl.program_id(1)
    tt = hT_ref.shape[1]

    @pl.when(e == 0)
    def _():
        o_ref[...] = jnp.zeros(o_ref.shape, F32)

    act_ref[...] = _dot_mixed(u_ref[...], hT_ref[...])
    gate_tokens = min(GATE_TOKENS, tt)
    for tp in range(tt // gate_tokens):
        ts = slice(tp * gate_tokens, (tp + 1) * gate_tokens)
        for ii in range(PEER_EC // N_KEYS):
            i = e * (PEER_EC // N_KEYS) + ii
            g = None
            for h in range(PEER_HEADS):
                nrow = _bcast_row_bf16(n_ref, h, i, ts)
                arow = _bcast_row_bf16(a_ref, h, i, ts)
                gate = e1_ref[h, :, :, ts] * arow[None]
                val = jnp.where(r1_ref[h, :, :, ts] < nrow[None], gate, jnp.zeros_like(gate))
                g = val if g is None else g + val
            es = slice(ii * N_KEYS, (ii + 1) * N_KEYS)
            act = act_ref[es, ts]
            ge = (0.5 * act * (1.0 + lax.erf(act * (1.0 / math.sqrt(2.0))))).astype(BF16)
            w_ref[ts, es] = (ge * g.reshape(N_KEYS, gate_tokens)).T
    o_ref[...] += _dot_mixed(w_ref[...], v_ref[...])


def peer_dense(hT, n, a, r1, e1, u_tab, v_tab, layer):
    D, M = hT.shape
    tt = min(M, 1024)
    groups = N_KEYS // BF16_SUBLANES
    r1 = r1.reshape(PEER_HEADS, groups, BF16_SUBLANES, M)
    e1 = e1.reshape(PEER_HEADS, groups, BF16_SUBLANES, M)
    once = pl.Buffered(1)
    return pl.pallas_call(
        _peer_dense_kernel,
        grid=(M // tt, N_CHUNKS),
        in_specs=[
            pl.BlockSpec((D, tt), lambda t, e: (0, t), pipeline_mode=once),
            pl.BlockSpec((PEER_HEADS, N_KEYS, tt), lambda t, e: (0, 0, t), pipeline_mode=once),
            pl.BlockSpec((PEER_HEADS, N_KEYS, tt), lambda t, e: (0, 0, t), pipeline_mode=once),
            pl.BlockSpec((PEER_HEADS, groups, BF16_SUBLANES, tt), lambda t, e: (0, 0, 0, t), pipeline_mode=once),
            pl.BlockSpec((PEER_HEADS, groups, BF16_SUBLANES, tt), lambda t, e: (0, 0, 0, t), pipeline_mode=once),
            pl.BlockSpec((None, PEER_EC, D), lambda t, e: (layer, e, 0)),
            pl.BlockSpec((None, PEER_EC, D), lambda t, e: (layer, e, 0)),
        ],
        out_specs=pl.BlockSpec((tt, D), lambda t, e: (t, 0), pipeline_mode=once),
        out_shape=jax.ShapeDtypeStruct((M, D), F32),
        scratch_shapes=[pltpu.VMEM((PEER_EC, tt), F32), pltpu.VMEM((tt, PEER_EC), BF16)],
        compiler_params=_cparams(("arbitrary", "arbitrary")),
        name="peer_dense",
    )(hT, n, a, r1, e1, u_tab, v_tab)


def _t5_bucket(rel):
    n = jnp.maximum(rel, 0)
    max_exact = NUM_BUCKETS // 2
    nf = jnp.maximum(n, 1).astype(F32)
    large = max_exact + (jnp.log(nf / max_exact) / math.log(MAX_DISTANCE / max_exact)
                         * (NUM_BUCKETS - max_exact)).astype(jnp.int32)
    large = jnp.minimum(large, NUM_BUCKETS - 1)
    return jnp.where(n < max_exact, n, large)


def _bias_tables(rel_bias):
    rb = rel_bias.astype(F32)
    per_rel = rb[_t5_bucket(jnp.arange(WINDOW, dtype=jnp.int32))].T
    span = 3 * WINDOW - 1
    line = jnp.full((N_HEADS, span), NEG, F32).at[:, 1:WINDOW + 1].set(per_rel[:, ::-1])
    skew = jnp.tile(line, (1, WINDOW))[:, :WINDOW * (span - 1)].reshape(N_HEADS, WINDOW, span - 1)
    tab = skew[:, :, :2 * WINDOW]
    bias_p = (tab.reshape(N_KV_HEADS, PAIRS, 2, WINDOW, 2 * WINDOW)
              .transpose(0, 2, 1, 3, 4).reshape(N_KV_HEADS, 2, PAIRS * WINDOW, 2 * WINDOW))
    bias_p = bias_p * LOG2E
    no_prev = jnp.arange(2 * WINDOW) < WINDOW
    bias_p = jnp.stack([bias_p, jnp.where(no_prev, NEG * LOG2E, bias_p)])
    rel_s = WINDOW - jnp.arange(WINDOW, dtype=jnp.int32)
    tab_s = jnp.where((rel_s < WINDOW)[:, None], rb[_t5_bucket(rel_s)], NEG)
    bias_s = tab_s.T
    bias0 = rb[_t5_bucket(jnp.zeros((1,), jnp.int32))].T
    return bias_p, bias_s, bias0


def _peer_layer(x, g_ffn_i, sc, sh, gt, w_pq, sub_keys, u_tab, v_tab, i):
    G, R, D = x.shape
    h, hT = norm_mod(x, g_ffn_i, sc, sh, transposed=True)
    q = matmul(h.reshape(G * R, D), w_pq, i)
    n, a, r1, e1 = peer_topk(q, sub_keys, i)
    y = peer_dense(hT, n, a, r1, e1, u_tab, v_tab, i)
    return y.reshape(G, R, D), gt


def _split_mod(mod):
    return [mod[:, :, k * D_MODEL:(k + 1) * D_MODEL] for k in range(6)]


def kernel(x_prompt, x_sample, cache_k, cache_v, state_conv, c_prompt, c_sample, rel_bias,
           w_ada, b_ada, g_mix, g_ffn, g_final, w_qkv, b_qkv, w_o, sinks,
           w_pw1, b_pw1, w_dw, b_dw, ln_g, ln_b, w_pw2, b_pw2, w_pq, sub_keys, u_tab, v_tab):
    B, T, D = x_prompt.shape
    S = x_sample.shape[0]
    n_attn = w_qkv.shape[0]

    rows = B + S
    rows_pad = -(-rows // 16) * 16
    c_all = jnp.concatenate([c_prompt, c_sample, jnp.zeros((rows_pad - rows, D), F32)], axis=0)
    mod = ada_mod(c_all, w_ada, b_ada)
    bias_p, bias_s, bias0 = _bias_tables(rel_bias)

    xp = x_prompt
    xs = jnp.pad(x_sample.reshape(1, S, D), ((0, 0), (0, SAMPLE_PAD - S), (0, 0)))
    new_kp, new_vp, new_cp, new_ks, new_vs, new_cs = [], [], [], [], [], []
    tm_rows = T

    for i in range(DEPTH):
        mp = _split_mod(mod[i, :B].reshape(B, 1, 6 * D))
        ms = _split_mod(jnp.pad(mod[i, B:B + S], ((0, SAMPLE_PAD - S), (0, 0))).reshape(1, SAMPLE_PAD, 6 * D))
        if i == 0:
            hp = norm_mod(xp, g_mix[i], mp[1], mp[0])
            hs = norm_mod(xs, g_mix[i], ms[1], ms[0])
        else:
            xp, hp = norm_mod(xp, g_mix[i], mp[1], mp[0], resid=peer_p)
            xs, hs = norm_mod(xs, g_mix[i], ms[1], ms[0], resid=peer_s)
        hp = hp.reshape(B * T, D)
        hs = hs.reshape(SAMPLE_PAD, D)
        if i % 2 == 0:
            a = i // 2
            sink_a = sinks[a].astype(F32)
            qkv = matmul(hp, w_qkv, a, b_qkv)
            o = attn_prompt(qkv, bias_p, sink_a, T)
            xp = matmul(o, w_o, a, mode="resid", xres=xp.reshape(B * T, D), gate=mp[2],
                        rows_per_gate=tm_rows).reshape(B, T, D)
            kv = qkv.reshape(B, T, Q_DIM + 2 * KV_DIM)[:, T - WINDOW:, Q_DIM:]
            new_kp.append(kv[..., :KV_DIM].reshape(B, WINDOW, N_KV_HEADS, HEAD_DIM))
            new_vp.append(kv[..., KV_DIM:].reshape(B, WINDOW, N_KV_HEADS, HEAD_DIM))
            qkv_s = matmul(hs, w_qkv, a, b_qkv)
            q3 = qkv_s[:S, :Q_DIM].reshape(S, N_HEADS, HEAD_DIM)
            k_new = qkv_s[:S, Q_DIM:Q_DIM + KV_DIM]
            v_new = qkv_s[:S, Q_DIM + KV_DIM:]
            ck = cache_k[a].reshape(S, WINDOW, KV_DIM)
            cv = cache_v[a].reshape(S, WINDOW, KV_DIM)
            o_s = attn_sample(q3, k_new.reshape(S, 1, KV_DIM), v_new.reshape(S, 1, KV_DIM), ck, cv,
                              bias_s, bias0, sink_a.reshape(N_HEADS, 1))
            o_s = jnp.pad(o_s.reshape(S, Q_DIM), ((0, SAMPLE_PAD - S), (0, 0))).astype(BF16)
            xs = matmul(o_s, w_o, a, mode="resid",
                        xres=xs.reshape(SAMPLE_PAD, D), gate=ms[2]).reshape(1, SAMPLE_PAD, D)
            new_ks.append(jnp.concatenate([ck[:, 1:], k_new[:, None]], axis=1)
                          .reshape(S, WINDOW, N_KV_HEADS, HEAD_DIM))
            new_vs.append(jnp.concatenate([cv[:, 1:], v_new[:, None]], axis=1)
                          .reshape(S, WINDOW, N_KV_HEADS, HEAD_DIM))
        else:
            bl = i // 2
            u = matmul(hp, w_pw1, bl, b_pw1, mode="glu", n_out=D).reshape(B, T, D)
            y = conv_prompt(u, w_dw, b_dw, bl)
            z = ln_swish(y.reshape(B * T, D), ln_g, ln_b, bl)
            xp = matmul(z, w_pw2, bl, b_pw2, mode="resid", xres=xp.reshape(B * T, D), gate=mp[2],
                        rows_per_gate=tm_rows).reshape(B, T, D)
            new_cp.append(u[:, T - (CONV_WIDTH - 1):])
            u_s = matmul(hs, w_pw1, bl, b_pw1, mode="glu", n_out=D)
            st = state_conv[bl]
            y_s = conv_sample(st.transpose(1, 0, 2), u_s[:S], w_dw, b_dw, bl)
            z_s = ln_swish(jnp.pad(y_s, ((0, SAMPLE_PAD - S), (0, 0))), ln_g, ln_b, bl)
            xs = matmul(z_s, w_pw2, bl, b_pw2, mode="resid", xres=xs.reshape(SAMPLE_PAD, D),
                        gate=ms[2]).reshape(1, SAMPLE_PAD, D)
            new_cs.append(jnp.concatenate([st[:, 1:], u_s[:S, None]], axis=1))
        peer_p = _peer_layer(xp, g_ffn[i], mp[4], mp[3], mp[5], w_pq, sub_keys, u_tab, v_tab, i)
        peer_s = _peer_layer(xs, g_ffn[i], ms[4], ms[3], ms[5], w_pq, sub_keys, u_tab, v_tab, i)

    y_prompt = norm_mod(xp, g_final, out_dtype=F32, resid=peer_p, keep_x=False)
    y_sample = norm_mod(xs, g_final, out_dtype=F32, resid=peer_s, keep_x=False)[0, :S].reshape(S, 1, D)
    return (y_prompt, y_sample, jnp.stack(new_kp), jnp.stack(new_vp), jnp.stack(new_cp),
            jnp.stack(new_ks), jnp.stack(new_vs), jnp.stack(new_cs))
```

```python
import functools
import math

import jax
import jax.numpy as jnp
import numpy as np
from jax import lax
from jax.experimental import pallas as pl
from jax.experimental.pallas import tpu as pltpu

D_MODEL = 2048
DEPTH = 4
N_HEADS = 64
N_KV_HEADS = 8
HEAD_DIM = 64
GROUP = N_HEADS // N_KV_HEADS
Q_DIM = N_HEADS * HEAD_DIM
KV_DIM = N_KV_HEADS * HEAD_DIM
WINDOW = 128
NUM_BUCKETS = 32
MAX_DISTANCE = 128
CONV_WIDTH = 31
PEER_HEADS = 8
N_KEYS = 128
N_EXPERTS = N_KEYS * N_KEYS
PEER_TOPK = 16
D_KEY_HALF = 128
EPS = 1e-6
NEG = -1e30
LOG2E = 1.4426950408889634

VMEM_LIMIT_V7X = 56 * 1024 * 1024
SAMPLE_PAD = 128

BF16 = jnp.bfloat16
F32 = jnp.float32


def _cparams(sem, flags=None):
    return pltpu.CompilerParams(dimension_semantics=sem, vmem_limit_bytes=VMEM_LIMIT_V7X, flags=flags)


def _dot(a, b):
    return jnp.dot(a, b, preferred_element_type=F32)


def _dot_mixed(a, b):
    return lax.dot_general(a, b, (((1,), (0,)), ((), ())), preferred_element_type=F32)


def _dot_nt(a, b):
    return lax.dot_general(a, b, (((1,), (1,)), ((), ())), preferred_element_type=F32)


def _dot_tn(a, b):
    return lax.dot_general(a, b, (((0,), (0,)), ((), ())), preferred_element_type=F32)


def _ada_kernel(c_ref, w_ref, b_ref, o_ref):
    c = c_ref[...]
    cond = (c * jax.nn.sigmoid(c)).astype(BF16)
    o_ref[...] = _dot(cond, w_ref[...].astype(BF16)) + b_ref[...]


def ada_mod(c_all, w_ada, b_ada):
    rows = c_all.shape[0]
    n = w_ada.shape[2]
    tn = 1536
    return pl.pallas_call(
        _ada_kernel,
        grid=(DEPTH, n // tn),
        in_specs=[
            pl.BlockSpec((rows, D_MODEL), lambda l, j: (0, 0)),
            pl.BlockSpec((None, D_MODEL, tn), lambda l, j: (l, 0, j)),
            pl.BlockSpec((None, 1, tn), lambda l, j: (l, 0, j)),
        ],
        out_specs=pl.BlockSpec((None, rows, tn), lambda l, j: (l, 0, j)),
        out_shape=jax.ShapeDtypeStruct((DEPTH, rows, n), F32),
        compiler_params=_cparams(("arbitrary", "arbitrary")),
        name="ada_mod",
    )(c_all, w_ada, b_ada.reshape(DEPTH, 1, n))


def _norm_kernel(*refs, modulated, transposed, residual, keep_x):
    it = iter(refs)
    x_ref = next(it)
    if residual:
        y_ref, gate_ref = next(it), next(it)
    g_ref = next(it)
    if modulated:
        sc_ref, sh_ref = next(it), next(it)
    outs = list(it)
    x = x_ref[0]
    if residual:
        x = x + gate_ref[0] * y_ref[0]
        if keep_x:
            outs.pop(0)[0] = x
    y = x * lax.rsqrt(jnp.mean(x * x, axis=-1, keepdims=True) + EPS) * g_ref[...]
    if modulated:
        y = y * (1.0 + sc_ref[0]) + sh_ref[0]
    outs[0][0] = y.astype(outs[0].dtype)
    if transposed:
        outs[1][...] = y.T.astype(outs[1].dtype)


def norm_mod(x, g, sc=None, sh=None, *, out_dtype=BF16, transposed=False, resid=None, keep_x=True):
    G, R, D = x.shape
    tt = min(R, 512)
    nt = R // tt
    modulated = sc is not None
    residual = resid is not None
    row_spec = pl.BlockSpec((1, tt, D), lambda b, i: (b, i, 0))
    seq_spec = pl.BlockSpec((1, 1, D), lambda b, i: (b, 0, 0))
    per_rows = lambda arr: seq_spec if arr.shape[1] == 1 else row_spec
    in_specs = [row_spec]
    args = [x]
    if residual:
        y, gate = resid
        in_specs += [row_spec, per_rows(gate)]
        args += [y, gate]
    in_specs.append(pl.BlockSpec((1, D), lambda b, i: (0, 0)))
    args.append(g.reshape(1, D))
    if modulated:
        in_specs += [per_rows(sc), per_rows(sh)]
        args += [sc, sh]
    out_specs, out_shape = [], []
    if residual and keep_x:
        out_specs.append(row_spec)
        out_shape.append(jax.ShapeDtypeStruct((G, R, D), F32))
    out_specs.append(row_spec)
    out_shape.append(jax.ShapeDtypeStruct((G, R, D), out_dtype))
    if transposed:
        out_specs.append(pl.BlockSpec((D, tt), lambda b, i: (0, b * nt + i)))
        out_shape.append(jax.ShapeDtypeStruct((D, G * R), BF16))
    res = pl.pallas_call(
        functools.partial(_norm_kernel, modulated=modulated, transposed=transposed,
                          residual=residual, keep_x=keep_x),
        grid=(G, nt),
        in_specs=in_specs,
        out_specs=out_specs,
        out_shape=out_shape,
        compiler_params=_cparams(("arbitrary", "arbitrary")),
        name="norm_mod",
    )(*args)
    return res if len(res) > 1 else res[0]


def _mm_kernel(*refs, mode, has_bias):
    it = iter(refs)
    h_ref = next(it)
    w_ref = next(it)
    w2_ref = next(it) if mode == "glu" else None
    b_ref = next(it) if has_bias else None
    b2_ref = next(it) if (mode == "glu" and has_bias) else None
    if mode == "resid":
        x_ref = next(it)
        gate_ref = next(it)
    o_ref = next(it)
    ws_ref = next(it)
    ws2_ref = next(it) if mode == "glu" else None

    @pl.when(pl.program_id(1) == 0)
    def _():
        ws_ref[...] = w_ref[...].astype(BF16)
        if mode == "glu":
            ws2_ref[...] = w2_ref[...].astype(BF16)

    h = h_ref[...]
    acc = _dot(h, ws_ref[...])
    if has_bias:
        acc = acc + b_ref[...]
    if mode == "glu":
        gte = _dot(h, ws2_ref[...])
        if has_bias:
            gte = gte + b2_ref[...]
        acc = acc * jax.nn.sigmoid(gte)
    if mode == "resid":
        acc = x_ref[...] + gate_ref[0] * acc
    o_ref[...] = acc.astype(o_ref.dtype)


PROJ_VMEM_BUDGET = 48 * 1024 * 1024


def _proj_col_tile(tm, k, n, *, n_weights, out_bytes, resid):
    for tn in (1024, 512, 256, 128):
        if n % tn:
            continue
        need = (2 * tm * k * 2 + n_weights * (2 * k * tn * 4 + k * tn * 2)
                + 2 * tm * tn * out_bytes + (2 * tm * tn * 4 if resid else 0))
        if need <= PROJ_VMEM_BUDGET:
            return tn
    raise ValueError("no projection column tile fits VMEM")


def matmul(h, w, layer, bias=None, *, mode="plain", n_out=None, xres=None, gate=None,
           rows_per_gate=None, out_dtype=F32):
    M, K = h.shape
    nw = w.shape[2]
    N = n_out if n_out is not None else nw
    tm = min(M, 1024)
    tn = _proj_col_tile(tm, K, N, n_weights=2 if mode == "glu" else 1,
                        out_bytes=jnp.dtype(out_dtype).itemsize, resid=(mode == "resid"))
    nj, ni = N // tn, M // tm
    has_bias = bias is not None
    in_specs = [
        pl.BlockSpec((tm, K), lambda j, i: (i, 0)),
        pl.BlockSpec((None, K, tn), lambda j, i: (layer, 0, j)),
    ]
    args = [h, w]
    if mode == "glu":
        in_specs.append(pl.BlockSpec((None, K, tn), lambda j, i: (layer, 0, j + nj)))
        args.append(w)
    if has_bias:
        b3 = bias.reshape(bias.shape[0], 1, nw)
        in_specs.append(pl.BlockSpec((None, 1, tn), lambda j, i: (layer, 0, j)))
        args.append(b3)
        if mode == "glu":
            in_specs.append(pl.BlockSpec((None, 1, tn), lambda j, i: (layer, 0, j + nj)))
            args.append(b3)
    if mode == "resid":
        in_specs.append(pl.BlockSpec((tm, tn), lambda j, i: (i, j)))
        args.append(xres)
        if gate.shape[1] == 1:
            tiles_per_gate = rows_per_gate // tm
            in_specs.append(pl.BlockSpec((1, 1, tn), lambda j, i: (i // tiles_per_gate, 0, j)))
        else:
            in_specs.append(pl.BlockSpec((1, tm, tn), lambda j, i: (0, i, j)))
        args.append(gate)
    scratch = [pltpu.VMEM((K, tn), BF16)]
    if mode == "glu":
        scratch.append(pltpu.VMEM((K, tn), BF16))
    return pl.pallas_call(
        functools.partial(_mm_kernel, mode=mode, has_bias=has_bias),
        grid=(nj, ni),
        in_specs=in_specs,
        out_specs=pl.BlockSpec((tm, tn), lambda j, i: (i, j)),
        out_shape=jax.ShapeDtypeStruct((M, N), out_dtype),
        scratch_shapes=scratch,
        compiler_params=_cparams(("arbitrary", "arbitrary")),
        name="proj_" + mode,
    )(*args)


PAIRS = GROUP // 2


def _attn_prompt_kernel(q_ref, kc_ref, kp_ref, vc_ref, vp_ref, bias_ref, o_ref,
                        *, blocks_per_seq):
    which = ((pl.program_id(0) % blocks_per_seq) == 0).astype(jnp.int32)
    zeros = jnp.zeros((2 * WINDOW, HEAD_DIM), BF16)
    ones = jnp.ones((2 * WINDOW, 2 * HEAD_DIM), BF16)
    not_sink = lax.broadcasted_iota(jnp.int32, (2 * WINDOW, HEAD_DIM), 0) > 0
    for g in range(N_KV_HEADS):
        ks = slice(g * HEAD_DIM, (g + 1) * HEAD_DIM)
        kcat = jnp.concatenate([kp_ref[:, ks], kc_ref[:, ks]], axis=0)
        vcat = jnp.concatenate([vp_ref[:, ks], vc_ref[:, ks]], axis=0)
        kcat = jnp.where(not_sink, kcat, 0.0).astype(BF16)
        vcat = jnp.where(not_sink, vcat, 0.0).astype(BF16)
        qs = slice(g * GROUP * HEAD_DIM, (g + 1) * GROUP * HEAD_DIM)
        qg = jnp.concatenate(
            [q_ref[:, qs.start + pp * 2 * HEAD_DIM:qs.start + (pp + 1) * 2 * HEAD_DIM] for pp in range(PAIRS)],
            axis=0)
        qg = (qg * (HEAD_DIM ** -0.5 * LOG2E)).astype(BF16)
        out = None
        for par in range(2):
            kx = jnp.concatenate([kcat, zeros] if par == 0 else [zeros, kcat], axis=1)
            vx = jnp.concatenate(([vcat, zeros] if par == 0 else [zeros, vcat]) + [ones], axis=1)
            s = _dot_nt(qg, kx) + bias_ref[which, g, par]
            p = jnp.exp2(s - jnp.max(s, axis=-1, keepdims=True))
            od = _dot(p.astype(BF16), vx)
            o = od[:, :2 * HEAD_DIM] / od[:, 2 * HEAD_DIM:]
            out = o if out is None else out + o
        for pp in range(PAIRS):
            o_ref[:, qs.start + pp * 2 * HEAD_DIM:qs.start + (pp + 1) * 2 * HEAD_DIM] = (
                out[pp * WINDOW:(pp + 1) * WINDOW].astype(o_ref.dtype))


def attn_prompt(qkv, bias_tab, sinks, seq_len):
    sink_rows = jnp.repeat(sinks.astype(F32).reshape(N_KV_HEADS, PAIRS, 2).transpose(0, 2, 1), WINDOW, axis=2)
    is_sink_col = jnp.arange(2 * WINDOW) == 0
    bias_tab = jnp.where(is_sink_col, (sink_rows * LOG2E)[None, :, :, :, None], bias_tab)
    M = qkv.shape[0]
    nb = M // WINDOW
    bps = seq_len // WINDOW
    kcol = Q_DIM // KV_DIM
    prev = lambda r: jnp.maximum(r - 1, 0)
    return pl.pallas_call(
        functools.partial(_attn_prompt_kernel, blocks_per_seq=bps),
        grid=(nb,),
        in_specs=[
            pl.BlockSpec((WINDOW, Q_DIM), lambda r: (r, 0)),
            pl.BlockSpec((WINDOW, KV_DIM), lambda r: (r, kcol)),
            pl.BlockSpec((WINDOW, KV_DIM), lambda r: (prev(r), kcol)),
            pl.BlockSpec((WINDOW, KV_DIM), lambda r: (r, kcol + 1)),
            pl.BlockSpec((WINDOW, KV_DIM), lambda r: (prev(r), kcol + 1)),
            pl.BlockSpec((2, N_KV_HEADS, 2, PAIRS * WINDOW, 2 * WINDOW), lambda r: (0, 0, 0, 0, 0),
                         pipeline_mode=pl.Buffered(1)),
        ],
        out_specs=pl.BlockSpec((WINDOW, Q_DIM), lambda r: (r, 0)),
        out_shape=jax.ShapeDtypeStruct((M, Q_DIM), BF16),
        compiler_params=_cparams(("arbitrary",)),
        name="attn_prompt",
    )(qkv, qkv, qkv, qkv, qkv, bias_tab)


SAMPLE_SEQS_PER_STEP = 8


def _bdot(a, b, contract_a, contract_b):
    return lax.dot_general(a, b, (((contract_a,), (contract_b,)), ((0,), (0,))), preferred_element_type=F32)


def _attn_sample_kernel(q_ref, kn_ref, vn_ref, ck_ref, cv_ref, bias_ref, bias0_ref, sink_ref, o_ref):
    for g in range(N_KV_HEADS):
        ks = slice(g * HEAD_DIM, (g + 1) * HEAD_DIM)
        hs = slice(g * GROUP, (g + 1) * GROUP)
        qg = q_ref[:, hs, :] * (HEAD_DIM ** -0.5)
        kn = kn_ref[:, :, ks]
        vn = vn_ref[:, :, ks]
        s = _bdot(qg.astype(BF16), ck_ref[:, :, ks].astype(BF16), 2, 2) + bias_ref[hs, :][None]
        s_new = jnp.sum(qg * kn, axis=-1, keepdims=True) + bias0_ref[hs, :][None]
        sink = sink_ref[hs, :][None]
        m = jnp.maximum(jnp.maximum(jnp.max(s, axis=-1, keepdims=True), s_new), sink)
        p = jnp.exp(s - m)
        pn = jnp.exp(s_new - m)
        den = jnp.sum(p, axis=-1, keepdims=True) + pn + jnp.exp(sink - m)
        o = (_bdot(p.astype(BF16), cv_ref[:, :, ks].astype(BF16), 2, 1) + pn * vn) / den
        o_ref[:, hs, :] = o.astype(o_ref.dtype)


def attn_sample(q3, k_new, v_new, cache_k, cache_v, bias_s, bias0, sinks):
    nseq = cache_k.shape[0]
    sb = SAMPLE_SEQS_PER_STEP
    return pl.pallas_call(
        _attn_sample_kernel,
        grid=(nseq // sb,),
        in_specs=[
            pl.BlockSpec((sb, N_HEADS, HEAD_DIM), lambda b: (b, 0, 0)),
            pl.BlockSpec((sb, 1, KV_DIM), lambda b: (b, 0, 0)),
            pl.BlockSpec((sb, 1, KV_DIM), lambda b: (b, 0, 0)),
            pl.BlockSpec((sb, WINDOW, KV_DIM), lambda b: (b, 0, 0)),
            pl.BlockSpec((sb, WINDOW, KV_DIM), lambda b: (b, 0, 0)),
            pl.BlockSpec((N_HEADS, WINDOW), lambda b: (0, 0)),
            pl.BlockSpec((N_HEADS, 1), lambda b: (0, 0)),
            pl.BlockSpec((N_HEADS, 1), lambda b: (0, 0)),
        ],
        out_specs=pl.BlockSpec((sb, N_HEADS, HEAD_DIM), lambda b: (b, 0, 0)),
        out_shape=jax.ShapeDtypeStruct((nseq, N_HEADS, HEAD_DIM), F32),
        compiler_params=_cparams(("arbitrary",)),
        name="attn_sample",
    )(q3, k_new, v_new, cache_k, cache_v, bias_s, bias0, sinks)


CONV_HALO = 32
F32_SUBLANES = 8


def _conv_prompt_kernel(u_ref, halo_ref, w_ref, b_ref, o_ref, scr_ref, *, tt):
    first = pl.program_id(1) == 0

    @pl.when(first)
    def _():
        scr_ref[0:CONV_HALO, :] = jnp.zeros((CONV_HALO, scr_ref.shape[1]), F32)

    @pl.when(jnp.logical_not(first))
    def _():
        scr_ref[0:CONV_HALO, :] = halo_ref[0]

    scr_ref[CONV_HALO:, :] = u_ref[0]
    off = CONV_HALO - (CONV_WIDTH - 1)
    acc = jnp.broadcast_to(b_ref[...], (tt, scr_ref.shape[1]))
    for s in range(F32_SUBLANES):
        rows = tt if s == 0 else tt + F32_SUBLANES
        part = None
        for q in range(off, off + CONV_WIDTH):
            if q % F32_SUBLANES != s:
                continue
            term = scr_ref[q - s:q - s + rows, :] * w_ref[q - off:q - off + 1, :]
            part = term if part is None else part + term
        acc = acc + (part if s == 0 else part[s:s + tt])
    o_ref[0] = acc


def conv_prompt(u, w_dw, b_dw, layer):
    B, T, D = u.shape
    tt, dc = 512, 512
    hb = tt // CONV_HALO
    return pl.pallas_call(
        functools.partial(_conv_prompt_kernel, tt=tt),
        grid=(B, T // tt, D // dc),
        in_specs=[
            pl.BlockSpec((1, tt, dc), lambda b, i, j: (b, i, j)),
            pl.BlockSpec((1, CONV_HALO, dc), lambda b, i, j: (b, jnp.maximum(i * hb - 1, 0), j)),
            pl.BlockSpec((None, CONV_WIDTH, dc), lambda b, i, j: (layer, 0, j)),
            pl.BlockSpec((None, 1, dc), lambda b, i, j: (layer, 0, j)),
        ],
        out_specs=pl.BlockSpec((1, tt, dc), lambda b, i, j: (b, i, j)),
        out_shape=jax.ShapeDtypeStruct((B, T, D), F32),
        scratch_shapes=[pltpu.VMEM((CONV_HALO + tt, dc), F32)],
        compiler_params=_cparams(("arbitrary", "arbitrary", "arbitrary")),
        name="conv_prompt",
    )(u, u, w_dw, b_dw.reshape(b_dw.shape[0], 1, D))


def _conv_sample_kernel(st_ref, u_ref, w_ref, b_ref, o_ref):
    acc = u_ref[...] * w_ref[CONV_WIDTH - 1:CONV_WIDTH, :] + b_ref[...]
    for w in range(CONV_WIDTH - 1):
        acc = acc + st_ref[w] * w_ref[w:w + 1, :]
    o_ref[...] = acc


def conv_sample(state_t, u, w_dw, b_dw, layer):
    nseq, D = u.shape
    return pl.pallas_call(
        _conv_sample_kernel,
        grid=(1,),
        in_specs=[
            pl.BlockSpec((CONV_WIDTH - 1, nseq, D), lambda i: (0, 0, 0)),
            pl.BlockSpec((nseq, D), lambda i: (0, 0)),
            pl.BlockSpec((None, CONV_WIDTH, D), lambda i: (layer, 0, 0)),
            pl.BlockSpec((None, 1, D), lambda i: (layer, 0, 0)),
        ],
        out_specs=pl.BlockSpec((nseq, D), lambda i: (0, 0)),
        out_shape=jax.ShapeDtypeStruct((nseq, D), F32),
        compiler_params=_cparams(("arbitrary",)),
        name="conv_sample",
    )(state_t, u, w_dw, b_dw.reshape(b_dw.shape[0], 1, D))


def _ln_swish_kernel(y_ref, g_ref, b_ref, o_ref):
    y = y_ref[...]
    mu = jnp.mean(y, axis=-1, keepdims=True)
    yc = y - mu
    var = jnp.mean(yc * yc, axis=-1, keepdims=True)
    z = yc * lax.rsqrt(var + EPS) * g_ref[...] + b_ref[...]
    o_ref[...] = (z * jax.nn.sigmoid(z)).astype(o_ref.dtype)


def ln_swish(y, ln_g, ln_b, layer):
    M, D = y.shape
    tt = min(M, 512)
    return pl.pallas_call(
        _ln_swish_kernel,
        grid=(M // tt,),
        in_specs=[
            pl.BlockSpec((tt, D), lambda i: (i, 0)),
            pl.BlockSpec((None, 1, D), lambda i: (layer, 0, 0)),
            pl.BlockSpec((None, 1, D), lambda i: (layer, 0, 0)),
        ],
        out_specs=pl.BlockSpec((tt, D), lambda i: (i, 0)),
        out_shape=jax.ShapeDtypeStruct((M, D), BF16),
        compiler_params=_cparams(("arbitrary",)),
        name="ln_swish",
    )(y, ln_g.reshape(-1, 1, D), ln_b.reshape(-1, 1, D))


def _sorting_network(n):
    pairs = []

    def merge(lo, hi, r):
        step = 2 * r
        if step < hi - lo:
            merge(lo, hi, step)
            merge(lo + r, hi, step)
            pairs.extend((i, i + r) for i in range(lo + r, hi - r, step))
        else:
            pairs.append((lo, lo + r))

    def sort(lo, hi):
        if hi > lo:
            mid = lo + (hi - lo) // 2
            sort(lo, mid)
            sort(mid + 1, hi)
            merge(lo, hi, 1)

    sort(0, n - 1)
    return pairs


def _top_values_sorted(s, k, with_rank=False):
    groups = s.shape[0] // F32_SUBLANES
    lists = [s[r * F32_SUBLANES:(r + 1) * F32_SUBLANES] for r in range(groups)]
    for i, j in _sorting_network(pl.next_power_of_2(groups)):
        if j < groups:
            lists[i], lists[j] = jnp.maximum(lists[i], lists[j]), jnp.minimum(lists[i], lists[j])
    out = []
    for it in range(k):
        m = jnp.max(lists[0], axis=0, keepdims=True)
        out.append(m)
        live = min(groups, k - it) - 1
        if live <= 0:
            continue
        hit = lists[0] == m
        for d in range(live):
            lists[d] = jnp.where(hit, lists[d + 1], lists[d])
        if live == groups - 1:
            lists[live] = jnp.where(hit, -jnp.inf, lists[live])
    rank = None
    if with_rank:
        rank = jnp.full(s.shape, float(k), F32)
        for i in reversed(range(k)):
            rank = jnp.where(s >= out[i], float(i), rank)
    return out, rank


def _peer_topk_kernel(q_ref, keys_ref, n_ref, a_ref, r1_ref, e1_ref, top_ref):
    for h in range(PEER_HEADS):
        halves = []
        for p in range(2):
            hp = 2 * h + p
            qs = q_ref[:, hp * D_KEY_HALF:(hp + 1) * D_KEY_HALF].astype(BF16)
            s = _dot_nt(keys_ref[hp].astype(BF16), qs)
            tops, rank = _top_values_sorted(s, PEER_TOPK, with_rank=(p == 1))
            for k, m in enumerate(tops):
                top_ref[p, k:k + 1, :] = m
            halves.append((s, tops, rank))
        (s0, t0, _), (s1, t1, rank1) = halves
        t1_all = top_ref[1]
        pieces = ([t0[0] + t1_all] + [t0[a] + t1_all[0:8] for a in range(1, 8)]
                  + [top_ref[0, 8:PEER_TOPK, :] + t1[0]])
        best, _ = _top_values_sorted(jnp.concatenate(pieces, axis=0), PEER_TOPK)
        tau = best[-1]
        z = jnp.exp(best[0] - best[0])
        for v in best[1:]:
            z = z + jnp.exp(v - best[0])
        n = jnp.zeros(s0.shape, F32)
        for a in range(PEER_TOPK):
            cnt = jnp.sum(jnp.where(t0[a] + t1_all >= tau, 1.0, 0.0), axis=0, keepdims=True)
            n = jnp.where(s0 == t0[a], cnt, n)
        n_ref[h] = n
        a_ref[h] = jnp.exp(s0 - t0[0])
        r1_ref[h] = rank1.astype(BF16)
        e1_ref[h] = (jnp.exp(s1 - t1[0]) / z).astype(BF16)


def peer_topk(q, sub_keys, layer):
    M = q.shape[0]
    tt = min(M, 256)
    keys = sub_keys.reshape(DEPTH, PEER_HEADS * 2, N_KEYS, D_KEY_HALF)
    return pl.pallas_call(
        _peer_topk_kernel,
        grid=(M // tt,),
        in_specs=[
            pl.BlockSpec((tt, PEER_HEADS * 2 * D_KEY_HALF), lambda i: (i, 0)),
            pl.BlockSpec((None, PEER_HEADS * 2, N_KEYS, D_KEY_HALF), lambda i: (layer, 0, 0, 0)),
        ],
        out_specs=[pl.BlockSpec((PEER_HEADS, N_KEYS, tt), lambda i: (0, 0, i))] * 4,
        out_shape=[
            jax.ShapeDtypeStruct((PEER_HEADS, N_KEYS, M), F32),
            jax.ShapeDtypeStruct((PEER_HEADS, N_KEYS, M), F32),
            jax.ShapeDtypeStruct((PEER_HEADS, N_KEYS, M), BF16),
            jax.ShapeDtypeStruct((PEER_HEADS, N_KEYS, M), BF16),
        ],
        scratch_shapes=[pltpu.VMEM((2, PEER_TOPK, tt), F32)],
        compiler_params=_cparams(("arbitrary",)),
        name="peer_topk",
    )(q, keys)


PEER_EC = 512
PEER_EC_SMALL = 1024
PEER_TOKENS = 1024
BF16_SUBLANES = 16
GATE_TOKENS = 256


def _bcast_row_bf16(ref, h, i, ts):
    row = ref[h, pl.ds(i, 1), ts]
    rep = jnp.broadcast_to(row, (F32_SUBLANES, row.shape[1]))
    return jnp.concatenate([rep, rep], axis=0).astype(BF16)


def _peer_dense_kernel(hT_ref, n_ref, a_ref, r1_ref, e1_ref, u_ref, v_ref, o_ref,
                       act_ref, w_ref):
    e = pl.program_id(1)
    tt = hT_ref.shape[1]
    keys_per_chunk = u_ref.shape[0] // N_KEYS

    @pl.when(e == 0)
    def _():
        o_ref[...] = jnp.zeros(o_ref.shape, F32)

    act_ref[...] = _dot_mixed(u_ref[...], hT_ref[...])
    gate_tokens = min(GATE_TOKENS, tt)
    for tp in range(tt // gate_tokens):
        ts = slice(tp * gate_tokens, (tp + 1) * gate_tokens)
        for ii in range(keys_per_chunk):
            i = e * keys_per_chunk + ii
            g = None
            for h in range(PEER_HEADS):
                nrow = _bcast_row_bf16(n_ref, h, i, ts)
                arow = _bcast_row_bf16(a_ref, h, i, ts)
                gate = e1_ref[h, :, :, ts] * arow[None]
                val = jnp.where(r1_ref[h, :, :, ts] < nrow[None], gate, jnp.zeros_like(gate))
                g = val if g is None else g + val
            es = slice(ii * N_KEYS, (ii + 1) * N_KEYS)
            act = act_ref[es, ts]
            ge = (0.5 * act * (1.0 + lax.erf(act * (1.0 / math.sqrt(2.0))))).astype(BF16)
            w_ref[ts, es] = (ge * g.reshape(N_KEYS, gate_tokens)).T
    o_ref[...] += _dot_mixed(w_ref[...], v_ref[...])


def peer_dense(hT, n, a, r1, e1, u_tab, v_tab, layer):
    D, M = hT.shape
    tt = min(M, PEER_TOKENS)
    ec = PEER_EC if tt == PEER_TOKENS else PEER_EC_SMALL
    groups = N_KEYS // BF16_SUBLANES
    r1 = r1.reshape(PEER_HEADS, groups, BF16_SUBLANES, M)
    e1 = e1.reshape(PEER_HEADS, groups, BF16_SUBLANES, M)
    once = pl.Buffered(1)
    return pl.pallas_call(
        _peer_dense_kernel,
        grid=(M // tt, N_EXPERTS // ec),
        in_specs=[
            pl.BlockSpec((D, tt), lambda t, e: (0, t)),
            pl.BlockSpec((PEER_HEADS, N_KEYS, tt), lambda t, e: (0, 0, t), pipeline_mode=once),
            pl.BlockSpec((PEER_HEADS, N_KEYS, tt), lambda t, e: (0, 0, t), pipeline_mode=once),
            pl.BlockSpec((PEER_HEADS, groups, BF16_SUBLANES, tt), lambda t, e: (0, 0, 0, t)),
            pl.BlockSpec((PEER_HEADS, groups, BF16_SUBLANES, tt), lambda t, e: (0, 0, 0, t)),
            pl.BlockSpec((None, ec, D), lambda t, e: (layer, e, 0)),
            pl.BlockSpec((None, ec, D), lambda t, e: (layer, e, 0)),
        ],
        out_specs=pl.BlockSpec((tt, D), lambda t, e: (t, 0), pipeline_mode=once),
        out_shape=jax.ShapeDtypeStruct((M, D), F32),
        scratch_shapes=[pltpu.VMEM((ec, tt), F32), pltpu.VMEM((tt, ec), BF16)],
        compiler_params=_cparams(("arbitrary", "arbitrary")),
        name="peer_dense",
    )(hT, n, a, r1, e1, u_tab, v_tab)


def _t5_bucket(rel):
    n = jnp.maximum(rel, 0)
    max_exact = NUM_BUCKETS // 2
    nf = jnp.maximum(n, 1).astype(F32)
    large = max_exact + (jnp.log(nf / max_exact) / math.log(MAX_DISTANCE / max_exact)
                         * (NUM_BUCKETS - max_exact)).astype(jnp.int32)
    large = jnp.minimum(large, NUM_BUCKETS - 1)
    return jnp.where(n < max_exact, n, large)


def _bias_tables(rel_bias):
    rb = rel_bias.astype(F32)
    per_rel = rb[_t5_bucket(jnp.arange(WINDOW, dtype=jnp.int32))].T
    span = 3 * WINDOW - 1
    line = jnp.full((N_HEADS, span), NEG, F32).at[:, 1:WINDOW + 1].set(per_rel[:, ::-1])
    skew = jnp.tile(line, (1, WINDOW))[:, :WINDOW * (span - 1)].reshape(N_HEADS, WINDOW, span - 1)
    tab = skew[:, :, :2 * WINDOW]
    bias_p = (tab.reshape(N_KV_HEADS, PAIRS, 2, WINDOW, 2 * WINDOW)
              .transpose(0, 2, 1, 3, 4).reshape(N_KV_HEADS, 2, PAIRS * WINDOW, 2 * WINDOW))
    bias_p = bias_p * LOG2E
    no_prev = jnp.arange(2 * WINDOW) < WINDOW
    bias_p = jnp.stack([bias_p, jnp.where(no_prev, NEG * LOG2E, bias_p)])
    rel_s = WINDOW - jnp.arange(WINDOW, dtype=jnp.int32)
    tab_s = jnp.where((rel_s < WINDOW)[:, None], rb[_t5_bucket(rel_s)], NEG)
    bias_s = tab_s.T
    bias0 = rb[_t5_bucket(jnp.zeros((1,), jnp.int32))].T
    return bias_p, bias_s, bias0


def _peer_layer(x, g_ffn_i, sc, sh, gt, w_pq, sub_keys, u_tab, v_tab, i):
    G, R, D = x.shape
    h, hT = norm_mod(x, g_ffn_i, sc, sh, transposed=True)
    q = matmul(h.reshape(G * R, D), w_pq, i)
    n, a, r1, e1 = peer_topk(q, sub_keys, i)
    y = peer_dense(hT, n, a, r1, e1, u_tab, v_tab, i)
    return y.reshape(G, R, D), gt


def _split_mod(mod):
    return [mod[:, :, k * D_MODEL:(k + 1) * D_MODEL] for k in range(6)]


def kernel(x_prompt, x_sample, cache_k, cache_v, state_conv, c_prompt, c_sample, rel_bias,
           w_ada, b_ada, g_mix, g_ffn, g_final, w_qkv, b_qkv, w_o, sinks,
           w_pw1, b_pw1, w_dw, b_dw, ln_g, ln_b, w_pw2, b_pw2, w_pq, sub_keys, u_tab, v_tab):
    B, T, D = x_prompt.shape
    S = x_sample.shape[0]
    n_attn = w_qkv.shape[0]

    rows = B + S
    rows_pad = -(-rows // 16) * 16
    c_all = jnp.concatenate([c_prompt, c_sample, jnp.zeros((rows_pad - rows, D), F32)], axis=0)
    mod = ada_mod(c_all, w_ada, b_ada)
    bias_p, bias_s, bias0 = _bias_tables(rel_bias)

    xp = x_prompt
    xs = jnp.pad(x_sample.reshape(1, S, D), ((0, 0), (0, SAMPLE_PAD - S), (0, 0)))
    new_kp, new_vp, new_cp, new_ks, new_vs, new_cs = [], [], [], [], [], []
    tm_rows = T

    for i in range(DEPTH):
        mp = _split_mod(mod[i, :B].reshape(B, 1, 6 * D))
        ms = _split_mod(jnp.pad(mod[i, B:B + S], ((0, SAMPLE_PAD - S), (0, 0))).reshape(1, SAMPLE_PAD, 6 * D))
        if i == 0:
            hp = norm_mod(xp, g_mix[i], mp[1], mp[0])
            hs = norm_mod(xs, g_mix[i], ms[1], ms[0])
        else:
            xp, hp = norm_mod(xp, g_mix[i], mp[1], mp[0], resid=peer_p)
            xs, hs = norm_mod(xs, g_mix[i], ms[1], ms[0], resid=peer_s)
        hp = hp.reshape(B * T, D)
        hs = hs.reshape(SAMPLE_PAD, D)
        if i % 2 == 0:
            a = i // 2
            sink_a = sinks[a].astype(F32)
            qkv = matmul(hp, w_qkv, a, b_qkv)
            o = attn_prompt(qkv, bias_p, sink_a, T)
            xp = matmul(o, w_o, a, mode="resid", xres=xp.reshape(B * T, D), gate=mp[2],
                        rows_per_gate=tm_rows).reshape(B, T, D)
            kv = qkv.reshape(B, T, Q_DIM + 2 * KV_DIM)[:, T - WINDOW:, Q_DIM:]
            new_kp.append(kv[..., :KV_DIM].reshape(B, WINDOW, N_KV_HEADS, HEAD_DIM))
            new_vp.append(kv[..., KV_DIM:].reshape(B, WINDOW, N_KV_HEADS, HEAD_DIM))
            qkv_s = matmul(hs, w_qkv, a, b_qkv)
            q3 = qkv_s[:S, :Q_DIM].reshape(S, N_HEADS, HEAD_DIM)
            k_new = qkv_s[:S, Q_DIM:Q_DIM + KV_DIM]
            v_new = qkv_s[:S, Q_DIM + KV_DIM:]
            ck = cache_k[a].reshape(S, WINDOW, KV_DIM)
            cv = cache_v[a].reshape(S, WINDOW, KV_DIM)
            o_s = attn_sample(q3, k_new.reshape(S, 1, KV_DIM), v_new.reshape(S, 1, KV_DIM), ck, cv,
                              bias_s, bias0, sink_a.reshape(N_HEADS, 1))
            o_s = jnp.pad(o_s.reshape(S, Q_DIM), ((0, SAMPLE_PAD - S), (0, 0))).astype(BF16)
            xs = matmul(o_s, w_o, a, mode="resid",
                        xres=xs.reshape(SAMPLE_PAD, D), gate=ms[2]).reshape(1, SAMPLE_PAD, D)
            new_ks.append(jnp.concatenate([ck[:, 1:], k_new[:, None]], axis=1)
                          .reshape(S, WINDOW, N_KV_HEADS, HEAD_DIM))
            new_vs.append(jnp.concatenate([cv[:, 1:], v_new[:, None]], axis=1)
                          .reshape(S, WINDOW, N_KV_HEADS, HEAD_DIM))
        else:
            bl = i // 2
            u = matmul(hp, w_pw1, bl, b_pw1, mode="glu", n_out=D).reshape(B, T, D)
            y = conv_prompt(u, w_dw, b_dw, bl)
            z = ln_swish(y.reshape(B * T, D), ln_g, ln_b, bl)
            xp = matmul(z, w_pw2, bl, b_pw2, mode="resid", xres=xp.reshape(B * T, D), gate=mp[2],
                        rows_per_gate=tm_rows).reshape(B, T, D)
            new_cp.append(u[:, T - (CONV_WIDTH - 1):])
            u_s = matmul(hs, w_pw1, bl, b_pw1, mode="glu", n_out=D)
            st = state_conv[bl]
            y_s = conv_sample(st.transpose(1, 0, 2), u_s[:S], w_dw, b_dw, bl)
            z_s = ln_swish(jnp.pad(y_s, ((0, SAMPLE_PAD - S), (0, 0))), ln_g, ln_b, bl)
            xs = matmul(z_s, w_pw2, bl, b_pw2, mode="resid", xres=xs.reshape(SAMPLE_PAD, D),
                        gate=ms[2]).reshape(1, SAMPLE_PAD, D)
            new_cs.append(jnp.concatenate([st[:, 1:], u_s[:S, None]], axis=1))
        peer_p = _peer_layer(xp, g_ffn[i], mp[4], mp[3], mp[5], w_pq, sub_keys, u_tab, v_tab, i)
        peer_s = _peer_layer(xs, g_ffn[i], ms[4], ms[3], ms[5], w_pq, sub_keys, u_tab, v_tab, i)

    y_prompt = norm_mod(xp, g_final, out_dtype=F32, resid=peer_p, keep_x=False)
    y_sample = norm_mod(xs, g_final, out_dtype=F32, resid=peer_s, keep_x=False)[0, :S].reshape(S, 1, D)
    return (y_prompt, y_sample, jnp.stack(new_kp), jnp.stack(new_vp), jnp.stack(new_cp),
            jnp.stack(new_ks), jnp.stack(new_vs), jnp.stack(new_cs))
```

```python
import functools
import math

import jax
import jax.numpy as jnp
from jax import lax
from jax.experimental import pallas as pl
from jax.experimental.pallas import tpu as pltpu

D_MODEL = 2048
DEPTH = 4
N_HEADS = 64
N_KV_HEADS = 8
HEAD_DIM = 64
GROUP = N_HEADS // N_KV_HEADS
Q_DIM = N_HEADS * HEAD_DIM
KV_DIM = N_KV_HEADS * HEAD_DIM
WINDOW = 128
NUM_BUCKETS = 32
MAX_DISTANCE = 128
CONV_WIDTH = 31
PEER_HEADS = 8
N_KEYS = 128
N_EXPERTS = N_KEYS * N_KEYS
PEER_TOPK = 16
D_KEY_HALF = 128
EPS = 1e-6
NEG = -1e30
LOG2E = 1.4426950408889634

VMEM_LIMIT_V7X = 56 * 1024 * 1024
SAMPLE_PAD = 128

BF16 = jnp.bfloat16
F32 = jnp.float32


def _cparams(sem):
    return pltpu.CompilerParams(dimension_semantics=sem, vmem_limit_bytes=VMEM_LIMIT_V7X)


def _dot(a, b):
    return jnp.dot(a, b, preferred_element_type=F32)


def _dot_mixed(a, b):
    return lax.dot_general(a, b, (((1,), (0,)), ((), ())), preferred_element_type=F32)


def _dot_nt(a, b):
    return lax.dot_general(a, b, (((1,), (1,)), ((), ())), preferred_element_type=F32)


def _ada_kernel(c_ref, w_ref, b_ref, o_ref):
    c = c_ref[...]
    cond = (c * jax.nn.sigmoid(c)).astype(BF16)
    o_ref[...] = _dot(cond, w_ref[...].astype(BF16)) + b_ref[...]


def ada_mod(c_all, w_ada, b_ada):
    rows = c_all.shape[0]
    n = w_ada.shape[2]
    tn = 1536
    return pl.pallas_call(
        _ada_kernel,
        grid=(DEPTH, n // tn),
        in_specs=[
            pl.BlockSpec((rows, D_MODEL), lambda l, j: (0, 0)),
            pl.BlockSpec((None, D_MODEL, tn), lambda l, j: (l, 0, j)),
            pl.BlockSpec((None, 1, tn), lambda l, j: (l, 0, j)),
        ],
        out_specs=pl.BlockSpec((None, rows, tn), lambda l, j: (l, 0, j)),
        out_shape=jax.ShapeDtypeStruct((DEPTH, rows, n), F32),
        compiler_params=_cparams(("arbitrary", "arbitrary")),
        name="ada_mod",
    )(c_all, w_ada, b_ada.reshape(DEPTH, 1, n))


def _norm_kernel(*refs, modulated, transposed, residual, keep_x):
    it = iter(refs)
    x_ref = next(it)
    if residual:
        y_ref, gate_ref = next(it), next(it)
    g_ref = next(it)
    if modulated:
        sc_ref, sh_ref = next(it), next(it)
    outs = list(it)
    x = x_ref[0]
    if residual:
        x = x + gate_ref[0] * y_ref[0]
        if keep_x:
            outs.pop(0)[0] = x
    y = x * lax.rsqrt(jnp.mean(x * x, axis=-1, keepdims=True) + EPS) * g_ref[...]
    if modulated:
        y = y * (1.0 + sc_ref[0]) + sh_ref[0]
    outs[0][0] = y.astype(outs[0].dtype)
    if transposed:
        outs[1][...] = y.T.astype(outs[1].dtype)


def norm_mod(x, g, sc=None, sh=None, *, out_dtype=BF16, transposed=False, resid=None, keep_x=True):
    G, R, D = x.shape
    tt = min(R, 512)
    nt = R // tt
    modulated = sc is not None
    residual = resid is not None
    row_spec = pl.BlockSpec((1, tt, D), lambda b, i: (b, i, 0))
    seq_spec = pl.BlockSpec((1, 1, D), lambda b, i: (b, 0, 0))
    per_rows = lambda arr: seq_spec if arr.shape[1] == 1 else row_spec
    in_specs = [row_spec]
    args = [x]
    if residual:
        y, gate = resid
        in_specs += [row_spec, per_rows(gate)]
        args += [y, gate]
    in_specs.append(pl.BlockSpec((1, D), lambda b, i: (0, 0)))
    args.append(g.reshape(1, D))
    if modulated:
        in_specs += [per_rows(sc), per_rows(sh)]
        args += [sc, sh]
    out_specs, out_shape = [], []
    if residual and keep_x:
        out_specs.append(row_spec)
        out_shape.append(jax.ShapeDtypeStruct((G, R, D), F32))
    out_specs.append(row_spec)
    out_shape.append(jax.ShapeDtypeStruct((G, R, D), out_dtype))
    if transposed:
        out_specs.append(pl.BlockSpec((D, tt), lambda b, i: (0, b * nt + i)))
        out_shape.append(jax.ShapeDtypeStruct((D, G * R), BF16))
    res = pl.pallas_call(
        functools.partial(_norm_kernel, modulated=modulated, transposed=transposed,
                          residual=residual, keep_x=keep_x),
        grid=(G, nt),
        in_specs=in_specs,
        out_specs=out_specs,
        out_shape=out_shape,
        compiler_params=_cparams(("arbitrary", "arbitrary")),
        name="norm_mod",
    )(*args)
    return res if len(res) > 1 else res[0]


def _mm_kernel(*refs, mode, has_bias):
    it = iter(refs)
    h_ref = next(it)
    w_ref = next(it)
    w2_ref = next(it) if mode == "glu" else None
    b_ref = next(it) if has_bias else None
    b2_ref = next(it) if (mode == "glu" and has_bias) else None
    if mode == "resid":
        x_ref = next(it)
        gate_ref = next(it)
    o_ref = next(it)
    ws_ref = next(it)
    ws2_ref = next(it) if mode == "glu" else None

    @pl.when(pl.program_id(1) == 0)
    def _():
        ws_ref[...] = w_ref[...].astype(BF16)
        if mode == "glu":
            ws2_ref[...] = w2_ref[...].astype(BF16)

    h = h_ref[...]
    acc = _dot(h, ws_ref[...])
    if has_bias:
        acc = acc + b_ref[...]
    if mode == "glu":
        gte = _dot(h, ws2_ref[...])
        if has_bias:
            gte = gte + b2_ref[...]
        acc = acc * jax.nn.sigmoid(gte)
    if mode == "resid":
        acc = x_ref[...] + gate_ref[0] * acc
    o_ref[...] = acc.astype(o_ref.dtype)


PROJ_VMEM_BUDGET = 48 * 1024 * 1024


def _proj_col_tile(tm, k, n, *, n_weights, out_bytes, resid):
    for tn in (1024, 512, 256, 128):
        if n % tn:
            continue
        need = (2 * tm * k * 2 + n_weights * (2 * k * tn * 4 + k * tn * 2)
                + 2 * tm * tn * out_bytes + (2 * tm * tn * 4 if resid else 0))
        if need <= PROJ_VMEM_BUDGET:
            return tn
    raise ValueError("no projection column tile fits VMEM")


def matmul(h, w, layer, bias=None, *, mode="plain", n_out=None, xres=None, gate=None,
           rows_per_gate=None, out_dtype=F32):
    M, K = h.shape
    nw = w.shape[2]
    N = n_out if n_out is not None else nw
    tm = min(M, 1024)
    tn = _proj_col_tile(tm, K, N, n_weights=2 if mode == "glu" else 1,
                        out_bytes=jnp.dtype(out_dtype).itemsize, resid=(mode == "resid"))
    nj, ni = N // tn, M // tm
    has_bias = bias is not None
    in_specs = [
        pl.BlockSpec((tm, K), lambda j, i: (i, 0)),
        pl.BlockSpec((None, K, tn), lambda j, i: (layer, 0, j)),
    ]
    args = [h, w]
    if mode == "glu":
        in_specs.append(pl.BlockSpec((None, K, tn), lambda j, i: (layer, 0, j + nj)))
        args.append(w)
    if has_bias:
        b3 = bias.reshape(bias.shape[0], 1, nw)
        in_specs.append(pl.BlockSpec((None, 1, tn), lambda j, i: (layer, 0, j)))
        args.append(b3)
        if mode == "glu":
            in_specs.append(pl.BlockSpec((None, 1, tn), lambda j, i: (layer, 0, j + nj)))
            args.append(b3)
    if mode == "resid":
        in_specs.append(pl.BlockSpec((tm, tn), lambda j, i: (i, j)))
        args.append(xres)
        if gate.shape[1] == 1:
            tiles_per_gate = rows_per_gate // tm
            in_specs.append(pl.BlockSpec((1, 1, tn), lambda j, i: (i // tiles_per_gate, 0, j)))
        else:
            in_specs.append(pl.BlockSpec((1, tm, tn), lambda j, i: (0, i, j)))
        args.append(gate)
    scratch = [pltpu.VMEM((K, tn), BF16)]
    if mode == "glu":
        scratch.append(pltpu.VMEM((K, tn), BF16))
    return pl.pallas_call(
        functools.partial(_mm_kernel, mode=mode, has_bias=has_bias),
        grid=(nj, ni),
        in_specs=in_specs,
        out_specs=pl.BlockSpec((tm, tn), lambda j, i: (i, j)),
        out_shape=jax.ShapeDtypeStruct((M, N), out_dtype),
        scratch_shapes=scratch,
        compiler_params=_cparams(("arbitrary", "arbitrary")),
        name="proj_" + mode,
    )(*args)


PAIRS = GROUP // 2


def _attn_prompt_kernel(q_ref, kc_ref, kp_ref, vc_ref, vp_ref, bias_ref, o_ref,
                        *, blocks_per_seq):
    which = ((pl.program_id(0) % blocks_per_seq) == 0).astype(jnp.int32)
    zeros = jnp.zeros((2 * WINDOW, HEAD_DIM), BF16)
    ones = jnp.ones((2 * WINDOW, 2 * HEAD_DIM), BF16)
    not_sink = lax.broadcasted_iota(jnp.int32, (2 * WINDOW, HEAD_DIM), 0) > 0
    for g in range(N_KV_HEADS):
        ks = slice(g * HEAD_DIM, (g + 1) * HEAD_DIM)
        kcat = jnp.concatenate([kp_ref[:, ks], kc_ref[:, ks]], axis=0)
        vcat = jnp.concatenate([vp_ref[:, ks], vc_ref[:, ks]], axis=0)
        kcat = jnp.where(not_sink, kcat, 0.0).astype(BF16)
        vcat = jnp.where(not_sink, vcat, 0.0).astype(BF16)
        qs = slice(g * GROUP * HEAD_DIM, (g + 1) * GROUP * HEAD_DIM)
        qg = jnp.concatenate(
            [q_ref[:, qs.start + pp * 2 * HEAD_DIM:qs.start + (pp + 1) * 2 * HEAD_DIM] for pp in range(PAIRS)],
            axis=0)
        qg = (qg * (HEAD_DIM ** -0.5 * LOG2E)).astype(BF16)
        out = None
        for par in range(2):
            kx = jnp.concatenate([kcat, zeros] if par == 0 else [zeros, kcat], axis=1)
            vx = jnp.concatenate(([vcat, zeros] if par == 0 else [zeros, vcat]) + [ones], axis=1)
            s = _dot_nt(qg, kx) + bias_ref[which, g, par]
            p = jnp.exp2(s - jnp.max(s, axis=-1, keepdims=True))
            od = _dot(p.astype(BF16), vx)
            o = od[:, :2 * HEAD_DIM] / od[:, 2 * HEAD_DIM:]
            out = o if out is None else out + o
        for pp in range(PAIRS):
            o_ref[:, qs.start + pp * 2 * HEAD_DIM:qs.start + (pp + 1) * 2 * HEAD_DIM] = (
                out[pp * WINDOW:(pp + 1) * WINDOW].astype(o_ref.dtype))


def attn_prompt(qkv, bias_tab, sinks, seq_len):
    sink_rows = jnp.repeat(sinks.astype(F32).reshape(N_KV_HEADS, PAIRS, 2).transpose(0, 2, 1), WINDOW, axis=2)
    is_sink_col = jnp.arange(2 * WINDOW) == 0
    bias_tab = jnp.where(is_sink_col, (sink_rows * LOG2E)[None, :, :, :, None], bias_tab)
    M = qkv.shape[0]
    nb = M // WINDOW
    bps = seq_len // WINDOW
    kcol = Q_DIM // KV_DIM
    prev = lambda r: jnp.maximum(r - 1, 0)
    return pl.pallas_call(
        functools.partial(_attn_prompt_kernel, blocks_per_seq=bps),
        grid=(nb,),
        in_specs=[
            pl.BlockSpec((WINDOW, Q_DIM), lambda r: (r, 0)),
            pl.BlockSpec((WINDOW, KV_DIM), lambda r: (r, kcol)),
            pl.BlockSpec((WINDOW, KV_DIM), lambda r: (prev(r), kcol)),
            pl.BlockSpec((WINDOW, KV_DIM), lambda r: (r, kcol + 1)),
            pl.BlockSpec((WINDOW, KV_DIM), lambda r: (prev(r), kcol + 1)),
            pl.BlockSpec((2, N_KV_HEADS, 2, PAIRS * WINDOW, 2 * WINDOW), lambda r: (0, 0, 0, 0, 0),
                         pipeline_mode=pl.Buffered(1)),
        ],
        out_specs=pl.BlockSpec((WINDOW, Q_DIM), lambda r: (r, 0)),
        out_shape=jax.ShapeDtypeStruct((M, Q_DIM), BF16),
        compiler_params=_cparams(("arbitrary",)),
        name="attn_prompt",
    )(qkv, qkv, qkv, qkv, qkv, bias_tab)


SAMPLE_SEQS_PER_STEP = 8


def _bdot(a, b, contract_a, contract_b):
    return lax.dot_general(a, b, (((contract_a,), (contract_b,)), ((0,), (0,))), preferred_element_type=F32)


def _attn_sample_kernel(q_ref, kn_ref, vn_ref, ck_ref, cv_ref, bias_ref, bias0_ref, sink_ref, o_ref):
    for g in range(N_KV_HEADS):
        ks = slice(g * HEAD_DIM, (g + 1) * HEAD_DIM)
        hs = slice(g * GROUP, (g + 1) * GROUP)
        qg = q_ref[:, hs, :] * (HEAD_DIM ** -0.5)
        kn = kn_ref[:, :, ks]
        vn = vn_ref[:, :, ks]
        s = _bdot(qg.astype(BF16), ck_ref[:, :, ks].astype(BF16), 2, 2) + bias_ref[hs, :][None]
        s_new = jnp.sum(qg * kn, axis=-1, keepdims=True) + bias0_ref[hs, :][None]
        sink = sink_ref[hs, :][None]
        m = jnp.maximum(jnp.maximum(jnp.max(s, axis=-1, keepdims=True), s_new), sink)
        p = jnp.exp(s - m)
        pn = jnp.exp(s_new - m)
        den = jnp.sum(p, axis=-1, keepdims=True) + pn + jnp.exp(sink - m)
        o = (_bdot(p.astype(BF16), cv_ref[:, :, ks].astype(BF16), 2, 1) + pn * vn) / den
        o_ref[:, hs, :] = o.astype(o_ref.dtype)


def attn_sample(q3, k_new, v_new, cache_k, cache_v, bias_s, bias0, sinks):
    nseq = cache_k.shape[0]
    sb = SAMPLE_SEQS_PER_STEP
    return pl.pallas_call(
        _attn_sample_kernel,
        grid=(nseq // sb,),
        in_specs=[
            pl.BlockSpec((sb, N_HEADS, HEAD_DIM), lambda b: (b, 0, 0)),
            pl.BlockSpec((sb, 1, KV_DIM), lambda b: (b, 0, 0)),
            pl.BlockSpec((sb, 1, KV_DIM), lambda b: (b, 0, 0)),
            pl.BlockSpec((sb, WINDOW, KV_DIM), lambda b: (b, 0, 0)),
            pl.BlockSpec((sb, WINDOW, KV_DIM), lambda b: (b, 0, 0)),
            pl.BlockSpec((N_HEADS, WINDOW), lambda b: (0, 0)),
            pl.BlockSpec((N_HEADS, 1), lambda b: (0, 0)),
            pl.BlockSpec((N_HEADS, 1), lambda b: (0, 0)),
        ],
        out_specs=pl.BlockSpec((sb, N_HEADS, HEAD_DIM), lambda b: (b, 0, 0)),
        out_shape=jax.ShapeDtypeStruct((nseq, N_HEADS, HEAD_DIM), F32),
        compiler_params=_cparams(("arbitrary",)),
        name="attn_sample",
    )(q3, k_new, v_new, cache_k, cache_v, bias_s, bias0, sinks)


CONV_HALO = 32
F32_SUBLANES = 8


def _conv_prompt_kernel(u_ref, halo_ref, w_ref, b_ref, o_ref, scr_ref, *, tt):
    first = pl.program_id(1) == 0

    @pl.when(first)
    def _():
        scr_ref[0:CONV_HALO, :] = jnp.zeros((CONV_HALO, scr_ref.shape[1]), F32)

    @pl.when(jnp.logical_not(first))
    def _():
        scr_ref[0:CONV_HALO, :] = halo_ref[0]

    scr_ref[CONV_HALO:, :] = u_ref[0]
    off = CONV_HALO - (CONV_WIDTH - 1)
    acc = jnp.broadcast_to(b_ref[...], (tt, scr_ref.shape[1]))
    for s in range(F32_SUBLANES):
        rows = tt if s == 0 else tt + F32_SUBLANES
        part = None
        for q in range(off, off + CONV_WIDTH):
            if q % F32_SUBLANES != s:
                continue
            term = scr_ref[q - s:q - s + rows, :] * w_ref[q - off:q - off + 1, :]
            part = term if part is None else part + term
        acc = acc + (part if s == 0 else part[s:s + tt])
    o_ref[0] = acc


def conv_prompt(u, w_dw, b_dw, layer):
    B, T, D = u.shape
    tt, dc = 512, 512
    hb = tt // CONV_HALO
    return pl.pallas_call(
        functools.partial(_conv_prompt_kernel, tt=tt),
        grid=(B, T // tt, D // dc),
        in_specs=[
            pl.BlockSpec((1, tt, dc), lambda b, i, j: (b, i, j)),
            pl.BlockSpec((1, CONV_HALO, dc), lambda b, i, j: (b, jnp.maximum(i * hb - 1, 0), j)),
            pl.BlockSpec((None, CONV_WIDTH, dc), lambda b, i, j: (layer, 0, j)),
            pl.BlockSpec((None, 1, dc), lambda b, i, j: (layer, 0, j)),
        ],
        out_specs=pl.BlockSpec((1, tt, dc), lambda b, i, j: (b, i, j)),
        out_shape=jax.ShapeDtypeStruct((B, T, D), F32),
        scratch_shapes=[pltpu.VMEM((CONV_HALO + tt, dc), F32)],
        compiler_params=_cparams(("arbitrary", "arbitrary", "arbitrary")),
        name="conv_prompt",
    )(u, u, w_dw, b_dw.reshape(b_dw.shape[0], 1, D))


def _conv_sample_kernel(st_ref, u_ref, w_ref, b_ref, o_ref):
    acc = u_ref[...] * w_ref[CONV_WIDTH - 1:CONV_WIDTH, :] + b_ref[...]
    for w in range(CONV_WIDTH - 1):
        acc = acc + st_ref[w] * w_ref[w:w + 1, :]
    o_ref[...] = acc


def conv_sample(state_t, u, w_dw, b_dw, layer):
    nseq, D = u.shape
    return pl.pallas_call(
        _conv_sample_kernel,
        grid=(1,),
        in_specs=[
            pl.BlockSpec((CONV_WIDTH - 1, nseq, D), lambda i: (0, 0, 0)),
            pl.BlockSpec((nseq, D), lambda i: (0, 0)),
            pl.BlockSpec((None, CONV_WIDTH, D), lambda i: (layer, 0, 0)),
            pl.BlockSpec((None, 1, D), lambda i: (layer, 0, 0)),
        ],
        out_specs=pl.BlockSpec((nseq, D), lambda i: (0, 0)),
        out_shape=jax.ShapeDtypeStruct((nseq, D), F32),
        compiler_params=_cparams(("arbitrary",)),
        name="conv_sample",
    )(state_t, u, w_dw, b_dw.reshape(b_dw.shape[0], 1, D))


def _ln_swish_kernel(y_ref, g_ref, b_ref, o_ref):
    y = y_ref[...]
    mu = jnp.mean(y, axis=-1, keepdims=True)
    yc = y - mu
    var = jnp.mean(yc * yc, axis=-1, keepdims=True)
    z = yc * lax.rsqrt(var + EPS) * g_ref[...] + b_ref[...]
    o_ref[...] = (z * jax.nn.sigmoid(z)).astype(o_ref.dtype)


def ln_swish(y, ln_g, ln_b, layer):
    M, D = y.shape
    tt = min(M, 512)
    return pl.pallas_call(
        _ln_swish_kernel,
        grid=(M // tt,),
        in_specs=[
            pl.BlockSpec((tt, D), lambda i: (i, 0)),
            pl.BlockSpec((None, 1, D), lambda i: (layer, 0, 0)),
            pl.BlockSpec((None, 1, D), lambda i: (layer, 0, 0)),
        ],
        out_specs=pl.BlockSpec((tt, D), lambda i: (i, 0)),
        out_shape=jax.ShapeDtypeStruct((M, D), BF16),
        compiler_params=_cparams(("arbitrary",)),
        name="ln_swish",
    )(y, ln_g.reshape(-1, 1, D), ln_b.reshape(-1, 1, D))


def _sorting_network(n):
    pairs = []

    def merge(lo, hi, r):
        step = 2 * r
        if step < hi - lo:
            merge(lo, hi, step)
            merge(lo + r, hi, step)
            pairs.extend((i, i + r) for i in range(lo + r, hi - r, step))
        else:
            pairs.append((lo, lo + r))

    def sort(lo, hi):
        if hi > lo:
            mid = lo + (hi - lo) // 2
            sort(lo, mid)
            sort(mid + 1, hi)
            merge(lo, hi, 1)

    sort(0, n - 1)
    return pairs


def _top_values_sorted(s, k, with_rank=False):
    groups = s.shape[0] // F32_SUBLANES
    lists = [s[r * F32_SUBLANES:(r + 1) * F32_SUBLANES] for r in range(groups)]
    for i, j in _sorting_network(pl.next_power_of_2(groups)):
        if j < groups:
            lists[i], lists[j] = jnp.maximum(lists[i], lists[j]), jnp.minimum(lists[i], lists[j])
    out = []
    for it in range(k):
        m = jnp.max(lists[0], axis=0, keepdims=True)
        out.append(m)
        live = min(groups, k - it) - 1
        if live <= 0:
            continue
        hit = lists[0] == m
        for d in range(live):
            lists[d] = jnp.where(hit, lists[d + 1], lists[d])
        if live == groups - 1:
            lists[live] = jnp.where(hit, -jnp.inf, lists[live])
    rank = None
    if with_rank:
        rank = jnp.full(s.shape, float(k), F32)
        for i in reversed(range(k)):
            rank = jnp.where(s >= out[i], float(i), rank)
    return out, rank


def _peer_topk_kernel(q_ref, keys_ref, n_ref, a_ref, r1_ref, e1_ref, top_ref):
    for h in range(PEER_HEADS):
        halves = []
        for p in range(2):
            hp = 2 * h + p
            qs = q_ref[:, hp * D_KEY_HALF:(hp + 1) * D_KEY_HALF].astype(BF16)
            s = _dot_nt(keys_ref[hp].astype(BF16), qs)
            tops, rank = _top_values_sorted(s, PEER_TOPK, with_rank=(p == 1))
            for k, m in enumerate(tops):
                top_ref[p, k:k + 1, :] = m
            halves.append((s, tops, rank))
        (s0, t0, _), (s1, t1, rank1) = halves
        t1_all = top_ref[1]
        pieces = ([t0[0] + t1_all] + [t0[a] + t1_all[0:8] for a in range(1, 8)]
                  + [top_ref[0, 8:PEER_TOPK, :] + t1[0]])
        best, _ = _top_values_sorted(jnp.concatenate(pieces, axis=0), PEER_TOPK)
        tau = best[-1]
        z = jnp.exp(best[0] - best[0])
        for v in best[1:]:
            z = z + jnp.exp(v - best[0])
        n = jnp.zeros(s0.shape, F32)
        for a in range(PEER_TOPK):
            cnt = jnp.sum(jnp.where(t0[a] + t1_all >= tau, 1.0, 0.0), axis=0, keepdims=True)
            n = jnp.where(s0 == t0[a], cnt, n)
        n_ref[h] = n
        a_ref[h] = jnp.exp(s0 - t0[0])
        r1_ref[h] = rank1.astype(BF16)
        e1_ref[h] = (jnp.exp(s1 - t1[0]) / z).astype(BF16)


def peer_topk(q, sub_keys, layer):
    M = q.shape[0]
    tt = min(M, 256)
    keys = sub_keys.reshape(DEPTH, PEER_HEADS * 2, N_KEYS, D_KEY_HALF)
    return pl.pallas_call(
        _peer_topk_kernel,
        grid=(M // tt,),
        in_specs=[
            pl.BlockSpec((tt, PEER_HEADS * 2 * D_KEY_HALF), lambda i: (i, 0)),
            pl.BlockSpec((None, PEER_HEADS * 2, N_KEYS, D_KEY_HALF), lambda i: (layer, 0, 0, 0)),
        ],
        out_specs=[pl.BlockSpec((PEER_HEADS, N_KEYS, tt), lambda i: (0, 0, i))] * 4,
        out_shape=[
            jax.ShapeDtypeStruct((PEER_HEADS, N_KEYS, M), F32),
            jax.ShapeDtypeStruct((PEER_HEADS, N_KEYS, M), F32),
            jax.ShapeDtypeStruct((PEER_HEADS, N_KEYS, M), BF16),
            jax.ShapeDtypeStruct((PEER_HEADS, N_KEYS, M), BF16),
        ],
        scratch_shapes=[pltpu.VMEM((2, PEER_TOPK, tt), F32)],
        compiler_params=_cparams(("arbitrary",)),
        name="peer_topk",
    )(q, keys)


PEER_EC = 512
PEER_EC_SMALL = 1024
PEER_TOKENS = 1024
BF16_SUBLANES = 16
GATE_TOKENS = 256


def _bcast_row_bf16(ref, h, i, ts):
    row = ref[h, pl.ds(i, 1), ts]
    rep = jnp.broadcast_to(row, (F32_SUBLANES, row.shape[1]))
    return jnp.concatenate([rep, rep], axis=0).astype(BF16)


def _peer_dense_kernel(hT_ref, n_ref, a_ref, r1_ref, e1_ref, u_ref, v_ref, o_ref,
                       act_ref, w_ref):
    e = pl.program_id(1)
    tt = hT_ref.shape[1]
    keys_per_chunk = u_ref.shape[0] // N_KEYS

    @pl.when(e == 0)
    def _():
        o_ref[...] = jnp.zeros(o_ref.shape, F32)

    act_ref[...] = _dot_mixed(u_ref[...], hT_ref[...])
    gate_tokens = min(GATE_TOKENS, tt)
    for tp in range(tt // gate_tokens):
        ts = slice(tp * gate_tokens, (tp + 1) * gate_tokens)
        for ii in range(keys_per_chunk):
            i = e * keys_per_chunk + ii
            g = None
            for h in range(PEER_HEADS):
                nrow = _bcast_row_bf16(n_ref, h, i, ts)
                arow = _bcast_row_bf16(a_ref, h, i, ts)
                gate = e1_ref[h, :, :, ts] * arow[None]
                val = jnp.where(r1_ref[h, :, :, ts] < nrow[None], gate, jnp.zeros_like(gate))
                g = val if g is None else g + val
            es = slice(ii * N_KEYS, (ii + 1) * N_KEYS)
            act = act_ref[es, ts]
            ge = (0.5 * act * (1.0 + lax.erf(act * (1.0 / math.sqrt(2.0))))).astype(BF16)
            w_ref[ts, es] = (ge * g.reshape(N_KEYS, gate_tokens)).T
    o_ref[...] += _dot_mixed(w_ref[...], v_ref[...])


def peer_dense(hT, n, a, r1, e1, u_tab, v_tab, layer):
    D, M = hT.shape
    tt = min(M, PEER_TOKENS)
    ec = PEER_EC if tt == PEER_TOKENS else PEER_EC_SMALL
    groups = N_KEYS // BF16_SUBLANES
    r1 = r1.reshape(PEER_HEADS, groups, BF16_SUBLANES, M)
    e1 = e1.reshape(PEER_HEADS, groups, BF16_SUBLANES, M)
    once = pl.Buffered(1)
    return pl.pallas_call(
        _peer_dense_kernel,
        grid=(M // tt, N_EXPERTS // ec),
        in_specs=[
            pl.BlockSpec((D, tt), lambda t, e: (0, t)),
            pl.BlockSpec((PEER_HEADS, N_KEYS, tt), lambda t, e: (0, 0, t), pipeline_mode=once),
            pl.BlockSpec((PEER_HEADS, N_KEYS, tt), lambda t, e: (0, 0, t), pipeline_mode=once),
            pl.BlockSpec((PEER_HEADS, groups, BF16_SUBLANES, tt), lambda t, e: (0, 0, 0, t)),
            pl.BlockSpec((PEER_HEADS, groups, BF16_SUBLANES, tt), lambda t, e: (0, 0, 0, t)),
            pl.BlockSpec((None, ec, D), lambda t, e: (layer, e, 0)),
            pl.BlockSpec((None, ec, D), lambda t, e: (layer, e, 0)),
        ],
        out_specs=pl.BlockSpec((tt, D), lambda t, e: (t, 0), pipeline_mode=once),
        out_shape=jax.ShapeDtypeStruct((M, D), F32),
        scratch_shapes=[pltpu.VMEM((ec, tt), F32), pltpu.VMEM((tt, ec), BF16)],
        compiler_params=_cparams(("arbitrary", "arbitrary")),
        name="peer_dense",
    )(hT, n, a, r1, e1, u_tab, v_tab)


def _t5_bucket(rel):
    n = jnp.maximum(rel, 0)
    max_exact = NUM_BUCKETS // 2
    nf = jnp.maximum(n, 1).astype(F32)
    large = max_exact + (jnp.log(nf / max_exact) / math.log(MAX_DISTANCE / max_exact)
                         * (NUM_BUCKETS - max_exact)).astype(jnp.int32)
    large = jnp.minimum(large, NUM_BUCKETS - 1)
    return jnp.where(n < max_exact, n, large)


def _bias_tables(rel_bias):
    rb = rel_bias.astype(F32)
    per_rel = rb[_t5_bucket(jnp.arange(WINDOW, dtype=jnp.int32))].T
    span = 3 * WINDOW - 1
    line = jnp.full((N_HEADS, span), NEG, F32).at[:, 1:WINDOW + 1].set(per_rel[:, ::-1])
    skew = jnp.tile(line, (1, WINDOW))[:, :WINDOW * (span - 1)].reshape(N_HEADS, WINDOW, span - 1)
    tab = skew[:, :, :2 * WINDOW]
    bias_p = (tab.reshape(N_KV_HEADS, PAIRS, 2, WINDOW, 2 * WINDOW)
              .transpose(0, 2, 1, 3, 4).reshape(N_KV_HEADS, 2, PAIRS * WINDOW, 2 * WINDOW))
    bias_p = bias_p * LOG2E
    no_prev = jnp.arange(2 * WINDOW) < WINDOW
    bias_p = jnp.stack([bias_p, jnp.where(no_prev, NEG * LOG2E, bias_p)])
    rel_s = WINDOW - jnp.arange(WINDOW, dtype=jnp.int32)
    tab_s = jnp.where((rel_s < WINDOW)[:, None], rb[_t5_bucket(rel_s)], NEG)
    bias_s = tab_s.T
    bias0 = rb[_t5_bucket(jnp.zeros((1,), jnp.int32))].T
    return bias_p, bias_s, bias0


def _peer_layer(x, g_ffn_i, sc, sh, gt, w_pq, sub_keys, u_tab, v_tab, i):
    G, R, D = x.shape
    h, hT = norm_mod(x, g_ffn_i, sc, sh, transposed=True)
    q = matmul(h.reshape(G * R, D), w_pq, i)
    n, a, r1, e1 = peer_topk(q, sub_keys, i)
    y = peer_dense(hT, n, a, r1, e1, u_tab, v_tab, i)
    return y.reshape(G, R, D), gt


def _split_mod(mod):
    return [mod[:, :, k * D_MODEL:(k + 1) * D_MODEL] for k in range(6)]


def kernel(x_prompt, x_sample, cache_k, cache_v, state_conv, c_prompt, c_sample, rel_bias,
           w_ada, b_ada, g_mix, g_ffn, g_final, w_qkv, b_qkv, w_o, sinks,
           w_pw1, b_pw1, w_dw, b_dw, ln_g, ln_b, w_pw2, b_pw2, w_pq, sub_keys, u_tab, v_tab):
    B, T, D = x_prompt.shape
    S = x_sample.shape[0]
    n_attn = w_qkv.shape[0]

    rows = B + S
    rows_pad = -(-rows // 16) * 16
    c_all = jnp.concatenate([c_prompt, c_sample, jnp.zeros((rows_pad - rows, D), F32)], axis=0)
    mod = ada_mod(c_all, w_ada, b_ada)
    bias_p, bias_s, bias0 = _bias_tables(rel_bias)

    xp = x_prompt
    xs = jnp.pad(x_sample.reshape(1, S, D), ((0, 0), (0, SAMPLE_PAD - S), (0, 0)))
    new_kp, new_vp, new_cp, new_ks, new_vs, new_cs = [], [], [], [], [], []
    tm_rows = T

    for i in range(DEPTH):
        mp = _split_mod(mod[i, :B].reshape(B, 1, 6 * D))
        ms = _split_mod(jnp.pad(mod[i, B:B + S], ((0, SAMPLE_PAD - S), (0, 0))).reshape(1, SAMPLE_PAD, 6 * D))
        if i == 0:
            hp = norm_mod(xp, g_mix[i], mp[1], mp[0])
            hs = norm_mod(xs, g_mix[i], ms[1], ms[0])
        else:
            xp, hp = norm_mod(xp, g_mix[i], mp[1], mp[0], resid=peer_p)
            xs, hs = norm_mod(xs, g_mix[i], ms[1], ms[0], resid=peer_s)
        hp = hp.reshape(B * T, D)
        hs = hs.reshape(SAMPLE_PAD, D)
        if i % 2 == 0:
            a = i // 2
            sink_a = sinks[a].astype(F32)
            qkv = matmul(hp, w_qkv, a, b_qkv)
            o = attn_prompt(qkv, bias_p, sink_a, T)
            xp = matmul(o, w_o, a, mode="resid", xres=xp.reshape(B * T, D), gate=mp[2],
                        rows_per_gate=tm_rows).reshape(B, T, D)
            kv = qkv.reshape(B, T, Q_DIM + 2 * KV_DIM)[:, T - WINDOW:, Q_DIM:]
            new_kp.append(kv[..., :KV_DIM].reshape(B, WINDOW, N_KV_HEADS, HEAD_DIM))
            new_vp.append(kv[..., KV_DIM:].reshape(B, WINDOW, N_KV_HEADS, HEAD_DIM))
            qkv_s = matmul(hs, w_qkv, a, b_qkv)
            q3 = qkv_s[:S, :Q_DIM].reshape(S, N_HEADS, HEAD_DIM)
            k_new = qkv_s[:S, Q_DIM:Q_DIM + KV_DIM]
            v_new = qkv_s[:S, Q_DIM + KV_DIM:]
            ck = cache_k[a].reshape(S, WINDOW, KV_DIM)
            cv = cache_v[a].reshape(S, WINDOW, KV_DIM)
            o_s = attn_sample(q3, k_new.reshape(S, 1, KV_DIM), v_new.reshape(S, 1, KV_DIM), ck, cv,
                              bias_s, bias0, sink_a.reshape(N_HEADS, 1))
            o_s = jnp.pad(o_s.reshape(S, Q_DIM), ((0, SAMPLE_PAD - S), (0, 0))).astype(BF16)
            xs = matmul(o_s, w_o, a, mode="resid",
                        xres=xs.reshape(SAMPLE_PAD, D), gate=ms[2]).reshape(1, SAMPLE_PAD, D)
            new_ks.append(jnp.concatenate([ck[:, 1:], k_new[:, None]], axis=1)
                          .reshape(S, WINDOW, N_KV_HEADS, HEAD_DIM))
            new_vs.append(jnp.concatenate([cv[:, 1:], v_new[:, None]], axis=1)
                          .reshape(S, WINDOW, N_KV_HEADS, HEAD_DIM))
        else:
            bl = i // 2
            u = matmul(hp, w_pw1, bl, b_pw1, mode="glu", n_out=D).reshape(B, T, D)
            y = conv_prompt(u, w_dw, b_dw, bl)
            z = ln_swish(y.reshape(B * T, D), ln_g, ln_b, bl)
            xp = matmul(z, w_pw2, bl, b_pw2, mode="resid", xres=xp.reshape(B * T, D), gate=mp[2],
                        rows_per_gate=tm_rows).reshape(B, T, D)
            new_cp.append(u[:, T - (CONV_WIDTH - 1):])
            u_s = matmul(hs, w_pw1, bl, b_pw1, mode="glu", n_out=D)
            st = state_conv[bl]
            y_s = conv_sample(st.transpose(1, 0, 2), u_s[:S], w_dw, b_dw, bl)
            z_s = ln_swish(jnp.pad(y_s, ((0, SAMPLE_PAD - S), (0, 0))), ln_g, ln_b, bl)
            xs = matmul(z_s, w_pw2, bl, b_pw2, mode="resid", xres=xs.reshape(SAMPLE_PAD, D),
                        gate=ms[2]).reshape(1, SAMPLE_PAD, D)
            new_cs.append(jnp.concatenate([st[:, 1:], u_s[:S, None]], axis=1))
        peer_p = _peer_layer(xp, g_ffn[i], mp[4], mp[3], mp[5], w_pq, sub_keys, u_tab, v_tab, i)
        peer_s = _peer_layer(xs, g_ffn[i], ms[4], ms[3], ms[5], w_pq, sub_keys, u_tab, v_tab, i)

    y_prompt = norm_mod(xp, g_final, out_dtype=F32, resid=peer_p, keep_x=False)
    y_sample = norm_mod(xs, g_final, out_dtype=F32, resid=peer_s, keep_x=False)[0, :S].reshape(S, 1, D)
    return (y_prompt, y_sample, jnp.stack(new_kp), jnp.stack(new_vp), jnp.stack(new_cp),
            jnp.stack(new_ks), jnp.stack(new_vs), jnp.stack(new_cs))
```

```python
import functools
import math

import jax
import jax.numpy as jnp
from jax import lax
from jax.experimental import pallas as pl
from jax.experimental.pallas import tpu as pltpu

D_MODEL = 2048
DEPTH = 4
N_HEADS = 64
N_KV_HEADS = 8
HEAD_DIM = 64
GROUP = N_HEADS // N_KV_HEADS
Q_DIM = N_HEADS * HEAD_DIM
KV_DIM = N_KV_HEADS * HEAD_DIM
WINDOW = 128
NUM_BUCKETS = 32
MAX_DISTANCE = 128
CONV_WIDTH = 31
PEER_HEADS = 8
N_KEYS = 128
N_EXPERTS = N_KEYS * N_KEYS
PEER_TOPK = 16
D_KEY_HALF = 128
EPS = 1e-6
NEG = -1e30
LOG2E = 1.4426950408889634

VMEM_LIMIT_V7X = 56 * 1024 * 1024
SAMPLE_PAD = 128

BF16 = jnp.bfloat16
F32 = jnp.float32


def _cparams(sem):
    return pltpu.CompilerParams(dimension_semantics=sem, vmem_limit_bytes=VMEM_LIMIT_V7X)


def _dot(a, b):
    return jnp.dot(a, b, preferred_element_type=F32)


def _dot_mixed(a, b):
    return lax.dot_general(a, b, (((1,), (0,)), ((), ())), preferred_element_type=F32)


def _dot_nt(a, b):
    return lax.dot_general(a, b, (((1,), (1,)), ((), ())), preferred_element_type=F32)


def _ada_kernel(c_ref, w_ref, b_ref, o_ref):
    c = c_ref[...]
    cond = (c * jax.nn.sigmoid(c)).astype(BF16)
    o_ref[...] = _dot(cond, w_ref[...].astype(BF16)) + b_ref[...]


def ada_mod(c_all, w_ada, b_ada):
    rows = c_all.shape[0]
    n = w_ada.shape[2]
    tn = 1536
    return pl.pallas_call(
        _ada_kernel,
        grid=(DEPTH, n // tn),
        in_specs=[
            pl.BlockSpec((rows, D_MODEL), lambda l, j: (0, 0)),
            pl.BlockSpec((None, D_MODEL, tn), lambda l, j: (l, 0, j)),
            pl.BlockSpec((None, 1, tn), lambda l, j: (l, 0, j)),
        ],
        out_specs=pl.BlockSpec((None, rows, tn), lambda l, j: (l, 0, j)),
        out_shape=jax.ShapeDtypeStruct((DEPTH, rows, n), F32),
        compiler_params=_cparams(("arbitrary", "arbitrary")),
        name="ada_mod",
    )(c_all, w_ada, b_ada.reshape(DEPTH, 1, n))


def _norm_kernel(*refs, modulated, transposed, residual, keep_x):
    it = iter(refs)
    x_ref = next(it)
    if residual:
        y_ref, gate_ref = next(it), next(it)
    g_ref = next(it)
    if modulated:
        sc_ref, sh_ref = next(it), next(it)
    outs = list(it)
    x = x_ref[0]
    if residual:
        x = x + gate_ref[0] * y_ref[0]
        if keep_x:
            outs.pop(0)[0] = x
    y = x * lax.rsqrt(jnp.mean(x * x, axis=-1, keepdims=True) + EPS) * g_ref[...]
    if modulated:
        y = y * (1.0 + sc_ref[0]) + sh_ref[0]
    outs[0][0] = y.astype(outs[0].dtype)
    if transposed:
        outs[1][...] = y.T.astype(outs[1].dtype)


def norm_mod(x, g, sc=None, sh=None, *, out_dtype=BF16, transposed=False, resid=None, keep_x=True):
    G, R, D = x.shape
    tt = min(R, 512)
    nt = R // tt
    modulated = sc is not None
    residual = resid is not None
    row_spec = pl.BlockSpec((1, tt, D), lambda b, i: (b, i, 0))
    seq_spec = pl.BlockSpec((1, 1, D), lambda b, i: (b, 0, 0))
    per_rows = lambda arr: seq_spec if arr.shape[1] == 1 else row_spec
    in_specs = [row_spec]
    args = [x]
    if residual:
        y, gate = resid
        in_specs += [row_spec, per_rows(gate)]
        args += [y, gate]
    in_specs.append(pl.BlockSpec((1, D), lambda b, i: (0, 0)))
    args.append(g.reshape(1, D))
    if modulated:
        in_specs += [per_rows(sc), per_rows(sh)]
        args += [sc, sh]
    out_specs, out_shape = [], []
    if residual and keep_x:
        out_specs.append(row_spec)
        out_shape.append(jax.ShapeDtypeStruct((G, R, D), F32))
    out_specs.append(row_spec)
    out_shape.append(jax.ShapeDtypeStruct((G, R, D), out_dtype))
    if transposed:
        out_specs.append(pl.BlockSpec((D, tt), lambda b, i: (0, b * nt + i)))
        out_shape.append(jax.ShapeDtypeStruct((D, G * R), BF16))
    res = pl.pallas_call(
        functools.partial(_norm_kernel, modulated=modulated, transposed=transposed,
                          residual=residual, keep_x=keep_x),
        grid=(G, nt),
        in_specs=in_specs,
        out_specs=out_specs,
        out_shape=out_shape,
        compiler_params=_cparams(("arbitrary", "arbitrary")),
        name="norm_mod",
    )(*args)
    return res if len(res) > 1 else res[0]


def _mm_kernel(*refs, mode, has_bias):
    it = iter(refs)
    h_ref = next(it)
    w_ref = next(it)
    w2_ref = next(it) if mode == "glu" else None
    b_ref = next(it) if has_bias else None
    b2_ref = next(it) if (mode == "glu" and has_bias) else None
    if mode == "resid":
        x_ref = next(it)
        gate_ref = next(it)
    o_ref = next(it)
    ws_ref = next(it)
    ws2_ref = next(it) if mode == "glu" else None

    @pl.when(pl.program_id(1) == 0)
    def _():
        ws_ref[...] = w_ref[...].astype(BF16)
        if mode == "glu":
            ws2_ref[...] = w2_ref[...].astype(BF16)

    h = h_ref[...]
    acc = _dot(h, ws_ref[...])
    if has_bias:
        acc = acc + b_ref[...]
    if mode == "glu":
        gte = _dot(h, ws2_ref[...])
        if has_bias:
            gte = gte + b2_ref[...]
        acc = acc * jax.nn.sigmoid(gte)
    if mode == "resid":
        acc = x_ref[...] + gate_ref[0] * acc
    o_ref[...] = acc.astype(o_ref.dtype)


PROJ_VMEM_BUDGET = 48 * 1024 * 1024


def _proj_col_tile(tm, k, n, *, n_weights, out_bytes, resid):
    for tn in (1024, 512, 256, 128):
        if n % tn:
            continue
        need = (2 * tm * k * 2 + n_weights * (2 * k * tn * 4 + k * tn * 2)
                + 2 * tm * tn * out_bytes + (2 * tm * tn * 4 if resid else 0))
        if need <= PROJ_VMEM_BUDGET:
            return tn
    raise ValueError("no projection column tile fits VMEM")


def matmul(h, w, layer, bias=None, *, mode="plain", n_out=None, xres=None, gate=None,
           rows_per_gate=None, out_dtype=F32):
    M, K = h.shape
    nw = w.shape[2]
    N = n_out if n_out is not None else nw
    tm = min(M, 1024)
    tn = _proj_col_tile(tm, K, N, n_weights=2 if mode == "glu" else 1,
                        out_bytes=jnp.dtype(out_dtype).itemsize, resid=(mode == "resid"))
    nj, ni = N // tn, M // tm
    has_bias = bias is not None
    in_specs = [
        pl.BlockSpec((tm, K), lambda j, i: (i, 0)),
        pl.BlockSpec((None, K, tn), lambda j, i: (layer, 0, j)),
    ]
    args = [h, w]
    if mode == "glu":
        in_specs.append(pl.BlockSpec((None, K, tn), lambda j, i: (layer, 0, j + nj)))
        args.append(w)
    if has_bias:
        b3 = bias.reshape(bias.shape[0], 1, nw)
        in_specs.append(pl.BlockSpec((None, 1, tn), lambda j, i: (layer, 0, j)))
        args.append(b3)
        if mode == "glu":
            in_specs.append(pl.BlockSpec((None, 1, tn), lambda j, i: (layer, 0, j + nj)))
            args.append(b3)
    if mode == "resid":
        in_specs.append(pl.BlockSpec((tm, tn), lambda j, i: (i, j)))
        args.append(xres)
        if gate.shape[1] == 1:
            tiles_per_gate = rows_per_gate // tm
            in_specs.append(pl.BlockSpec((1, 1, tn), lambda j, i: (i // tiles_per_gate, 0, j)))
        else:
            in_specs.append(pl.BlockSpec((1, tm, tn), lambda j, i: (0, i, j)))
        args.append(gate)
    scratch = [pltpu.VMEM((K, tn), BF16)]
    if mode == "glu":
        scratch.append(pltpu.VMEM((K, tn), BF16))
    return pl.pallas_call(
        functools.partial(_mm_kernel, mode=mode, has_bias=has_bias),
        grid=(nj, ni),
        in_specs=in_specs,
        out_specs=pl.BlockSpec((tm, tn), lambda j, i: (i, j)),
        out_shape=jax.ShapeDtypeStruct((M, N), out_dtype),
        scratch_shapes=scratch,
        compiler_params=_cparams(("arbitrary", "arbitrary")),
        name="proj_" + mode,
    )(*args)


PAIRS = GROUP // 2


def _attn_prompt_kernel(q_ref, kc_ref, kp_ref, vc_ref, vp_ref, bias_ref, o_ref,
                        *, blocks_per_seq):
    which = ((pl.program_id(0) % blocks_per_seq) == 0).astype(jnp.int32)
    zeros = jnp.zeros((2 * WINDOW, HEAD_DIM), BF16)
    ones = jnp.ones((2 * WINDOW, 2 * HEAD_DIM), BF16)
    not_sink = lax.broadcasted_iota(jnp.int32, (2 * WINDOW, HEAD_DIM), 0) > 0
    for g in range(N_KV_HEADS):
        ks = slice(g * HEAD_DIM, (g + 1) * HEAD_DIM)
        kcat = jnp.concatenate([kp_ref[:, ks], kc_ref[:, ks]], axis=0)
        vcat = jnp.concatenate([vp_ref[:, ks], vc_ref[:, ks]], axis=0)
        kcat = jnp.where(not_sink, kcat, 0.0).astype(BF16)
        vcat = jnp.where(not_sink, vcat, 0.0).astype(BF16)
        qs = slice(g * GROUP * HEAD_DIM, (g + 1) * GROUP * HEAD_DIM)
        qg = jnp.concatenate(
            [q_ref[:, qs.start + pp * 2 * HEAD_DIM:qs.start + (pp + 1) * 2 * HEAD_DIM] for pp in range(PAIRS)],
            axis=0)
        qg = (qg * (HEAD_DIM ** -0.5 * LOG2E)).astype(BF16)
        out = None
        for par in range(2):
            kx = jnp.concatenate([kcat, zeros] if par == 0 else [zeros, kcat], axis=1)
            vx = jnp.concatenate(([vcat, zeros] if par == 0 else [zeros, vcat]) + [ones], axis=1)
            s = _dot_nt(qg, kx) + bias_ref[which, g, par]
            p = jnp.exp2(s - jnp.max(s, axis=-1, keepdims=True))
            od = _dot(p.astype(BF16), vx)
            o = od[:, :2 * HEAD_DIM] / od[:, 2 * HEAD_DIM:]
            out = o if out is None else out + o
        for pp in range(PAIRS):
            o_ref[:, qs.start + pp * 2 * HEAD_DIM:qs.start + (pp + 1) * 2 * HEAD_DIM] = (
                out[pp * WINDOW:(pp + 1) * WINDOW].astype(o_ref.dtype))


def attn_prompt(qkv, bias_tab, sinks, seq_len):
    sink_rows = jnp.repeat(sinks.astype(F32).reshape(N_KV_HEADS, PAIRS, 2).transpose(0, 2, 1), WINDOW, axis=2)
    is_sink_col = jnp.arange(2 * WINDOW) == 0
    bias_tab = jnp.where(is_sink_col, (sink_rows * LOG2E)[None, :, :, :, None], bias_tab)
    M = qkv.shape[0]
    nb = M // WINDOW
    bps = seq_len // WINDOW
    kcol = Q_DIM // KV_DIM
    prev = lambda r: jnp.maximum(r - 1, 0)
    return pl.pallas_call(
        functools.partial(_attn_prompt_kernel, blocks_per_seq=bps),
        grid=(nb,),
        in_specs=[
            pl.BlockSpec((WINDOW, Q_DIM), lambda r: (r, 0)),
            pl.BlockSpec((WINDOW, KV_DIM), lambda r: (r, kcol)),
            pl.BlockSpec((WINDOW, KV_DIM), lambda r: (prev(r), kcol)),
            pl.BlockSpec((WINDOW, KV_DIM), lambda r: (r, kcol + 1)),
            pl.BlockSpec((WINDOW, KV_DIM), lambda r: (prev(r), kcol + 1)),
            pl.BlockSpec((2, N_KV_HEADS, 2, PAIRS * WINDOW, 2 * WINDOW), lambda r: (0, 0, 0, 0, 0),
                         pipeline_mode=pl.Buffered(1)),
        ],
        out_specs=pl.BlockSpec((WINDOW, Q_DIM), lambda r: (r, 0)),
        out_shape=jax.ShapeDtypeStruct((M, Q_DIM), BF16),
        compiler_params=_cparams(("arbitrary",)),
        name="attn_prompt",
    )(qkv, qkv, qkv, qkv, qkv, bias_tab)


SAMPLE_SEQS_PER_STEP = 8


def _bdot(a, b, contract_a, contract_b):
    return lax.dot_general(a, b, (((contract_a,), (contract_b,)), ((0,), (0,))), preferred_element_type=F32)


def _attn_sample_kernel(q_ref, kn_ref, vn_ref, ck_ref, cv_ref, bias_ref, bias0_ref, sink_ref, o_ref):
    for g in range(N_KV_HEADS):
        ks = slice(g * HEAD_DIM, (g + 1) * HEAD_DIM)
        hs = slice(g * GROUP, (g + 1) * GROUP)
        qg = q_ref[:, hs, :] * (HEAD_DIM ** -0.5)
        kn = kn_ref[:, :, ks]
        vn = vn_ref[:, :, ks]
        s = _bdot(qg.astype(BF16), ck_ref[:, :, ks].astype(BF16), 2, 2) + bias_ref[hs, :][None]
        s_new = jnp.sum(qg * kn, axis=-1, keepdims=True) + bias0_ref[hs, :][None]
        sink = sink_ref[hs, :][None]
        m = jnp.maximum(jnp.maximum(jnp.max(s, axis=-1, keepdims=True), s_new), sink)
        p = jnp.exp(s - m)
        pn = jnp.exp(s_new - m)
        den = jnp.sum(p, axis=-1, keepdims=True) + pn + jnp.exp(sink - m)
        o = (_bdot(p.astype(BF16), cv_ref[:, :, ks].astype(BF16), 2, 1) + pn * vn) / den
        o_ref[:, hs, :] = o.astype(o_ref.dtype)


def attn_sample(q3, k_new, v_new, cache_k, cache_v, bias_s, bias0, sinks):
    nseq = cache_k.shape[0]
    sb = SAMPLE_SEQS_PER_STEP
    return pl.pallas_call(
        _attn_sample_kernel,
        grid=(nseq // sb,),
        in_specs=[
            pl.BlockSpec((sb, N_HEADS, HEAD_DIM), lambda b: (b, 0, 0)),
            pl.BlockSpec((sb, 1, KV_DIM), lambda b: (b, 0, 0)),
            pl.BlockSpec((sb, 1, KV_DIM), lambda b: (b, 0, 0)),
            pl.BlockSpec((sb, WINDOW, KV_DIM), lambda b: (b, 0, 0)),
            pl.BlockSpec((sb, WINDOW, KV_DIM), lambda b: (b, 0, 0)),
            pl.BlockSpec((N_HEADS, WINDOW), lambda b: (0, 0)),
            pl.BlockSpec((N_HEADS, 1), lambda b: (0, 0)),
            pl.BlockSpec((N_HEADS, 1), lambda b: (0, 0)),
        ],
        out_specs=pl.BlockSpec((sb, N_HEADS, HEAD_DIM), lambda b: (b, 0, 0)),
        out_shape=jax.ShapeDtypeStruct((nseq, N_HEADS, HEAD_DIM), F32),
        compiler_params=_cparams(("arbitrary",)),
        name="attn_sample",
    )(q3, k_new, v_new, cache_k, cache_v, bias_s, bias0, sinks)


CONV_HALO = 32
F32_SUBLANES = 8


def _conv_prompt_kernel(u_ref, halo_ref, w_ref, b_ref, o_ref, scr_ref, *, tt):
    first = pl.program_id(1) == 0

    @pl.when(first)
    def _():
        scr_ref[0:CONV_HALO, :] = jnp.zeros((CONV_HALO, scr_ref.shape[1]), F32)

    @pl.when(jnp.logical_not(first))
    def _():
        scr_ref[0:CONV_HALO, :] = halo_ref[0]

    scr_ref[CONV_HALO:, :] = u_ref[0]
    off = CONV_HALO - (CONV_WIDTH - 1)
    acc = jnp.broadcast_to(b_ref[...], (tt, scr_ref.shape[1]))
    for s in range(F32_SUBLANES):
        rows = tt if s == 0 else tt + F32_SUBLANES
        part = None
        for q in range(off, off + CONV_WIDTH):
            if q % F32_SUBLANES != s:
                continue
            term = scr_ref[q - s:q - s + rows, :] * w_ref[q - off:q - off + 1, :]
            part = term if part is None else part + term
        acc = acc + (part if s == 0 else part[s:s + tt])
    o_ref[0] = acc


def conv_prompt(u, w_dw, b_dw, layer):
    B, T, D = u.shape
    tt, dc = 512, 512
    hb = tt // CONV_HALO
    return pl.pallas_call(
        functools.partial(_conv_prompt_kernel, tt=tt),
        grid=(B, T // tt, D // dc),
        in_specs=[
            pl.BlockSpec((1, tt, dc), lambda b, i, j: (b, i, j)),
            pl.BlockSpec((1, CONV_HALO, dc), lambda b, i, j: (b, jnp.maximum(i * hb - 1, 0), j)),
            pl.BlockSpec((None, CONV_WIDTH, dc), lambda b, i, j: (layer, 0, j)),
            pl.BlockSpec((None, 1, dc), lambda b, i, j: (layer, 0, j)),
        ],
        out_specs=pl.BlockSpec((1, tt, dc), lambda b, i, j: (b, i, j)),
        out_shape=jax.ShapeDtypeStruct((B, T, D), F32),
        scratch_shapes=[pltpu.VMEM((CONV_HALO + tt, dc), F32)],
        compiler_params=_cparams(("arbitrary", "arbitrary", "arbitrary")),
        name="conv_prompt",
    )(u, u, w_dw, b_dw.reshape(b_dw.shape[0], 1, D))


def _conv_sample_kernel(st_ref, u_ref, w_ref, b_ref, o_ref):
    acc = u_ref[...] * w_ref[CONV_WIDTH - 1:CONV_WIDTH, :] + b_ref[...]
    for w in range(CONV_WIDTH - 1):
        acc = acc + st_ref[w] * w_ref[w:w + 1, :]
    o_ref[...] = acc


def conv_sample(state_t, u, w_dw, b_dw, layer):
    nseq, D = u.shape
    return pl.pallas_call(
        _conv_sample_kernel,
        grid=(1,),
        in_specs=[
            pl.BlockSpec((CONV_WIDTH - 1, nseq, D), lambda i: (0, 0, 0)),
            pl.BlockSpec((nseq, D), lambda i: (0, 0)),
            pl.BlockSpec((None, CONV_WIDTH, D), lambda i: (layer, 0, 0)),
            pl.BlockSpec((None, 1, D), lambda i: (layer, 0, 0)),
        ],
        out_specs=pl.BlockSpec((nseq, D), lambda i: (0, 0)),
        out_shape=jax.ShapeDtypeStruct((nseq, D), F32),
        compiler_params=_cparams(("arbitrary",)),
        name="conv_sample",
    )(state_t, u, w_dw, b_dw.reshape(b_dw.shape[0], 1, D))


def _ln_swish_kernel(y_ref, g_ref, b_ref, o_ref):
    y = y_ref[...]
    mu = jnp.mean(y, axis=-1, keepdims=True)
    yc = y - mu
    var = jnp.mean(yc * yc, axis=-1, keepdims=True)
    z = yc * lax.rsqrt(var + EPS) * g_ref[...] + b_ref[...]
    o_ref[...] = (z * jax.nn.sigmoid(z)).astype(o_ref.dtype)


def ln_swish(y, ln_g, ln_b, layer):
    M, D = y.shape
    tt = min(M, 512)
    return pl.pallas_call(
        _ln_swish_kernel,
        grid=(M // tt,),
        in_specs=[
            pl.BlockSpec((tt, D), lambda i: (i, 0)),
            pl.BlockSpec((None, 1, D), lambda i: (layer, 0, 0)),
            pl.BlockSpec((None, 1, D), lambda i: (layer, 0, 0)),
        ],
        out_specs=pl.BlockSpec((tt, D), lambda i: (i, 0)),
        out_shape=jax.ShapeDtypeStruct((M, D), BF16),
        compiler_params=_cparams(("arbitrary",)),
        name="ln_swish",
    )(y, ln_g.reshape(-1, 1, D), ln_b.reshape(-1, 1, D))


def _sorting_network(n):
    pairs = []

    def merge(lo, hi, r):
        step = 2 * r
        if step < hi - lo:
            merge(lo, hi, step)
            merge(lo + r, hi, step)
            pairs.extend((i, i + r) for i in range(lo + r, hi - r, step))
        else:
            pairs.append((lo, lo + r))

    def sort(lo, hi):
        if hi > lo:
            mid = lo + (hi - lo) // 2
            sort(lo, mid)
            sort(mid + 1, hi)
            merge(lo, hi, 1)

    sort(0, n - 1)
    return pairs


def _top_values_sorted(s, k, with_rank=False):
    groups = s.shape[0] // F32_SUBLANES
    lists = [s[r * F32_SUBLANES:(r + 1) * F32_SUBLANES] for r in range(groups)]
    for i, j in _sorting_network(pl.next_power_of_2(groups)):
        if j < groups:
            lists[i], lists[j] = jnp.maximum(lists[i], lists[j]), jnp.minimum(lists[i], lists[j])
    out = []
    for it in range(k):
        m = jnp.max(lists[0], axis=0, keepdims=True)
        out.append(m)
        live = min(groups, k - it) - 1
        if live <= 0:
            continue
        hit = lists[0] == m
        for d in range(live):
            lists[d] = jnp.where(hit, lists[d + 1], lists[d])
        if live == groups - 1:
            lists[live] = jnp.where(hit, -jnp.inf, lists[live])
    rank = None
    if with_rank:
        rank = jnp.full(s.shape, float(k), F32)
        for i in reversed(range(k)):
            rank = jnp.where(s >= out[i], float(i), rank)
    return out, rank


def _peer_topk_kernel(q_ref, keys_ref, n_ref, a_ref, r1_ref, e1_ref, top_ref):
    for h in range(PEER_HEADS):
        halves = []
        for p in range(2):
            hp = 2 * h + p
            qs = q_ref[:, hp * D_KEY_HALF:(hp + 1) * D_KEY_HALF].astype(BF16)
            s = _dot_nt(keys_ref[hp].astype(BF16), qs)
            tops, rank = _top_values_sorted(s, PEER_TOPK, with_rank=(p == 1))
            for k, m in enumerate(tops):
                top_ref[p, k:k + 1, :] = m
            halves.append((s, tops, rank))
        (s0, t0, _), (s1, t1, rank1) = halves
        t1_all = top_ref[1]
        pieces = ([t0[0] + t1_all] + [t0[a] + t1_all[0:8] for a in range(1, 8)]
                  + [top_ref[0, 8:PEER_TOPK, :] + t1[0]])
        best, _ = _top_values_sorted(jnp.concatenate(pieces, axis=0), PEER_TOPK)
        tau = best[-1]
        z = jnp.exp(best[0] - best[0])
        for v in best[1:]:
            z = z + jnp.exp(v - best[0])
        n = jnp.where(s0 + t1[0] >= tau, 1.0, 0.0)
        for a in range(PEER_TOPK // 2):
            cnt = jnp.sum(jnp.where(t0[a] + t1_all >= tau, 1.0, 0.0), axis=0, keepdims=True)
            n = jnp.where(s0 == t0[a], cnt, n)
        n_ref[h] = n
        a_ref[h] = jnp.exp(s0 - t0[0])
        r1_ref[h] = rank1.astype(BF16)
        e1_ref[h] = (jnp.exp(s1 - t1[0]) / z).astype(BF16)


def peer_topk(q, sub_keys, layer):
    M = q.shape[0]
    tt = min(M, 256)
    keys = sub_keys.reshape(DEPTH, PEER_HEADS * 2, N_KEYS, D_KEY_HALF)
    return pl.pallas_call(
        _peer_topk_kernel,
        grid=(M // tt,),
        in_specs=[
            pl.BlockSpec((tt, PEER_HEADS * 2 * D_KEY_HALF), lambda i: (i, 0)),
            pl.BlockSpec((None, PEER_HEADS * 2, N_KEYS, D_KEY_HALF), lambda i: (layer, 0, 0, 0)),
        ],
        out_specs=[pl.BlockSpec((PEER_HEADS, N_KEYS, tt), lambda i: (0, 0, i))] * 4,
        out_shape=[
            jax.ShapeDtypeStruct((PEER_HEADS, N_KEYS, M), F32),
            jax.ShapeDtypeStruct((PEER_HEADS, N_KEYS, M), F32),
            jax.ShapeDtypeStruct((PEER_HEADS, N_KEYS, M), BF16),
            jax.ShapeDtypeStruct((PEER_HEADS, N_KEYS, M), BF16),
        ],
        scratch_shapes=[pltpu.VMEM((2, PEER_TOPK, tt), F32)],
        compiler_params=_cparams(("arbitrary",)),
        name="peer_topk",
    )(q, keys)


PEER_EC = 512
PEER_EC_SMALL = 1024
PEER_TOKENS = 1024
BF16_SUBLANES = 16
GATE_TOKENS = 256


def _bcast_row_bf16(ref, h, i, ts):
    row = ref[h, pl.ds(i, 1), ts]
    rep = jnp.broadcast_to(row, (F32_SUBLANES, row.shape[1]))
    return jnp.concatenate([rep, rep], axis=0).astype(BF16)


def _peer_dense_kernel(hT_ref, n_ref, a_ref, r1_ref, e1_ref, u_ref, v_ref, o_ref,
                       act_ref, w_ref):
    e = pl.program_id(1)
    tt = hT_ref.shape[1]
    keys_per_chunk = u_ref.shape[0] // N_KEYS

    @pl.when(e == 0)
    def _():
        o_ref[...] = jnp.zeros(o_ref.shape, F32)

    act_ref[...] = _dot_mixed(u_ref[...], hT_ref[...])
    gate_tokens = min(GATE_TOKENS, tt)
    for tp in range(tt // gate_tokens):
        ts = slice(tp * gate_tokens, (tp + 1) * gate_tokens)
        for ii in range(keys_per_chunk):
            i = e * keys_per_chunk + ii
            g = None
            for h in range(PEER_HEADS):
                nrow = _bcast_row_bf16(n_ref, h, i, ts)
                arow = _bcast_row_bf16(a_ref, h, i, ts)
                gate = e1_ref[h, :, :, ts] * arow[None]
                val = jnp.where(r1_ref[h, :, :, ts] < nrow[None], gate, jnp.zeros_like(gate))
                g = val if g is None else g + val
            es = slice(ii * N_KEYS, (ii + 1) * N_KEYS)
            act = act_ref[es, ts]
            ge = (0.5 * act * (1.0 + lax.erf(act * (1.0 / math.sqrt(2.0))))).astype(BF16)
            w_ref[ts, es] = (ge * g.reshape(N_KEYS, gate_tokens)).T
    o_ref[...] += _dot_mixed(w_ref[...], v_ref[...])


def peer_dense(hT, n, a, r1, e1, u_tab, v_tab, layer):
    D, M = hT.shape
    tt = min(M, PEER_TOKENS)
    ec = PEER_EC if tt == PEER_TOKENS else PEER_EC_SMALL
    groups = N_KEYS // BF16_SUBLANES
    r1 = r1.reshape(PEER_HEADS, groups, BF16_SUBLANES, M)
    e1 = e1.reshape(PEER_HEADS, groups, BF16_SUBLANES, M)
    once = pl.Buffered(1)
    return pl.pallas_call(
        _peer_dense_kernel,
        grid=(M // tt, N_EXPERTS // ec),
        in_specs=[
            pl.BlockSpec((D, tt), lambda t, e: (0, t)),
            pl.BlockSpec((PEER_HEADS, N_KEYS, tt), lambda t, e: (0, 0, t), pipeline_mode=once),
            pl.BlockSpec((PEER_HEADS, N_KEYS, tt), lambda t, e: (0, 0, t), pipeline_mode=once),
            pl.BlockSpec((PEER_HEADS, groups, BF16_SUBLANES, tt), lambda t, e: (0, 0, 0, t)),
            pl.BlockSpec((PEER_HEADS, groups, BF16_SUBLANES, tt), lambda t, e: (0, 0, 0, t)),
            pl.BlockSpec((None, ec, D), lambda t, e: (layer, e, 0)),
            pl.BlockSpec((None, ec, D), lambda t, e: (layer, e, 0)),
        ],
        out_specs=pl.BlockSpec((tt, D), lambda t, e: (t, 0), pipeline_mode=once),
        out_shape=jax.ShapeDtypeStruct((M, D), F32),
        scratch_shapes=[pltpu.VMEM((ec, tt), F32), pltpu.VMEM((tt, ec), BF16)],
        compiler_params=_cparams(("arbitrary", "arbitrary")),
        name="peer_dense",
    )(hT, n, a, r1, e1, u_tab, v_tab)


def _t5_bucket(rel):
    n = jnp.maximum(rel, 0)
    max_exact = NUM_BUCKETS // 2
    nf = jnp.maximum(n, 1).astype(F32)
    large = max_exact + (jnp.log(nf / max_exact) / math.log(MAX_DISTANCE / max_exact)
                         * (NUM_BUCKETS - max_exact)).astype(jnp.int32)
    large = jnp.minimum(large, NUM_BUCKETS - 1)
    return jnp.where(n < max_exact, n, large)


def _bias_tables(rel_bias):
    rb = rel_bias.astype(F32)
    per_rel = rb[_t5_bucket(jnp.arange(WINDOW, dtype=jnp.int32))].T
    span = 3 * WINDOW - 1
    line = jnp.full((N_HEADS, span), NEG, F32).at[:, 1:WINDOW + 1].set(per_rel[:, ::-1])
    skew = jnp.tile(line, (1, WINDOW))[:, :WINDOW * (span - 1)].reshape(N_HEADS, WINDOW, span - 1)
    tab = skew[:, :, :2 * WINDOW]
    bias_p = (tab.reshape(N_KV_HEADS, PAIRS, 2, WINDOW, 2 * WINDOW)
              .transpose(0, 2, 1, 3, 4).reshape(N_KV_HEADS, 2, PAIRS * WINDOW, 2 * WINDOW))
    bias_p = bias_p * LOG2E
    no_prev = jnp.arange(2 * WINDOW) < WINDOW
    bias_p = jnp.stack([bias_p, jnp.where(no_prev, NEG * LOG2E, bias_p)])
    rel_s = WINDOW - jnp.arange(WINDOW, dtype=jnp.int32)
    tab_s = jnp.where((rel_s < WINDOW)[:, None], rb[_t5_bucket(rel_s)], NEG)
    bias_s = tab_s.T
    bias0 = rb[_t5_bucket(jnp.zeros((1,), jnp.int32))].T
    return bias_p, bias_s, bias0


def _peer_layer(x, g_ffn_i, sc, sh, gt, w_pq, sub_keys, u_tab, v_tab, i):
    G, R, D = x.shape
    h, hT = norm_mod(x, g_ffn_i, sc, sh, transposed=True)
    q = matmul(h.reshape(G * R, D), w_pq, i)
    n, a, r1, e1 = peer_topk(q, sub_keys, i)
    y = peer_dense(hT, n, a, r1, e1, u_tab, v_tab, i)
    return y.reshape(G, R, D), gt


def _split_mod(mod):
    return [mod[:, :, k * D_MODEL:(k + 1) * D_MODEL] for k in range(6)]


def kernel(x_prompt, x_sample, cache_k, cache_v, state_conv, c_prompt, c_sample, rel_bias,
           w_ada, b_ada, g_mix, g_ffn, g_final, w_qkv, b_qkv, w_o, sinks,
           w_pw1, b_pw1, w_dw, b_dw, ln_g, ln_b, w_pw2, b_pw2, w_pq, sub_keys, u_tab, v_tab):
    B, T, D = x_prompt.shape
    S = x_sample.shape[0]
    n_attn = w_qkv.shape[0]

    rows = B + S
    rows_pad = -(-rows // 16) * 16
    c_all = jnp.concatenate([c_prompt, c_sample, jnp.zeros((rows_pad - rows, D), F32)], axis=0)
    mod = ada_mod(c_all, w_ada, b_ada)
    bias_p, bias_s, bias0 = _bias_tables(rel_bias)

    xp = x_prompt
    xs = jnp.pad(x_sample.reshape(1, S, D), ((0, 0), (0, SAMPLE_PAD - S), (0, 0)))
    new_kp, new_vp, new_cp, new_ks, new_vs, new_cs = [], [], [], [], [], []
    tm_rows = T

    for i in range(DEPTH):
        mp = _split_mod(mod[i, :B].reshape(B, 1, 6 * D))
        ms = _split_mod(jnp.pad(mod[i, B:B + S], ((0, SAMPLE_PAD - S), (0, 0))).reshape(1, SAMPLE_PAD, 6 * D))
        if i == 0:
            hp = norm_mod(xp, g_mix[i], mp[1], mp[0])
            hs = norm_mod(xs, g_mix[i], ms[1], ms[0])
        else:
            xp, hp = norm_mod(xp, g_mix[i], mp[1], mp[0], resid=peer_p)
            xs, hs = norm_mod(xs, g_mix[i], ms[1], ms[0], resid=peer_s)
        hp = hp.reshape(B * T, D)
        hs = hs.reshape(SAMPLE_PAD, D)
        if i % 2 == 0:
            a = i // 2
            sink_a = sinks[a].astype(F32)
            qkv = matmul(hp, w_qkv, a, b_qkv)
            o = attn_prompt(qkv, bias_p, sink_a, T)
            xp = matmul(o, w_o, a, mode="resid", xres=xp.reshape(B * T, D), gate=mp[2],
                        rows_per_gate=tm_rows).reshape(B, T, D)
            kv = qkv.reshape(B, T, Q_DIM + 2 * KV_DIM)[:, T - WINDOW:, Q_DIM:]
            new_kp.append(kv[..., :KV_DIM].reshape(B, WINDOW, N_KV_HEADS, HEAD_DIM))
            new_vp.append(kv[..., KV_DIM:].reshape(B, WINDOW, N_KV_HEADS, HEAD_DIM))
            qkv_s = matmul(hs, w_qkv, a, b_qkv)
            q3 = qkv_s[:S, :Q_DIM].reshape(S, N_HEADS, HEAD_DIM)
            k_new = qkv_s[:S, Q_DIM:Q_DIM + KV_DIM]
            v_new = qkv_s[:S, Q_DIM + KV_DIM:]
            ck = cache_k[a].reshape(S, WINDOW, KV_DIM)
            cv = cache_v[a].reshape(S, WINDOW, KV_DIM)
            o_s = attn_sample(q3, k_new.reshape(S, 1, KV_DIM), v_new.reshape(S, 1, KV_DIM), ck, cv,
                              bias_s, bias0, sink_a.reshape(N_HEADS, 1))
            o_s = jnp.pad(o_s.reshape(S, Q_DIM), ((0, SAMPLE_PAD - S), (0, 0))).astype(BF16)
            xs = matmul(o_s, w_o, a, mode="resid",
                        xres=xs.reshape(SAMPLE_PAD, D), gate=ms[2]).reshape(1, SAMPLE_PAD, D)
            new_ks.append(jnp.concatenate([ck[:, 1:], k_new[:, None]], axis=1)
                          .reshape(S, WINDOW, N_KV_HEADS, HEAD_DIM))
            new_vs.append(jnp.concatenate([cv[:, 1:], v_new[:, None]], axis=1)
                          .reshape(S, WINDOW, N_KV_HEADS, HEAD_DIM))
        else:
            bl = i // 2
            u = matmul(hp, w_pw1, bl, b_pw1, mode="glu", n_out=D).reshape(B, T, D)
            y = conv_prompt(u, w_dw, b_dw, bl)
            z = ln_swish(y.reshape(B * T, D), ln_g, ln_b, bl)
            xp = matmul(z, w_pw2, bl, b_pw2, mode="resid", xres=xp.reshape(B * T, D), gate=mp[2],
                        rows_per_gate=tm_rows).reshape(B, T, D)
            new_cp.append(u[:, T - (CONV_WIDTH - 1):])
            u_s = matmul(hs, w_pw1, bl, b_pw1, mode="glu", n_out=D)
            st = state_conv[bl]
            y_s = conv_sample(st.transpose(1, 0, 2), u_s[:S], w_dw, b_dw, bl)
            z_s = ln_swish(jnp.pad(y_s, ((0, SAMPLE_PAD - S), (0, 0))), ln_g, ln_b, bl)
            xs = matmul(z_s, w_pw2, bl, b_pw2, mode="resid", xres=xs.reshape(SAMPLE_PAD, D),
                        gate=ms[2]).reshape(1, SAMPLE_PAD, D)
            new_cs.append(jnp.concatenate([st[:, 1:], u_s[:S, None]], axis=1))
        peer_p = _peer_layer(xp, g_ffn[i], mp[4], mp[3], mp[5], w_pq, sub_keys, u_tab, v_tab, i)
        peer_s = _peer_layer(xs, g_ffn[i], ms[4], ms[3], ms[5], w_pq, sub_keys, u_tab, v_tab, i)

    y_prompt = norm_mod(xp, g_final, out_dtype=F32, resid=peer_p, keep_x=False)
    y_sample = norm_mod(xs, g_final, out_dtype=F32, resid=peer_s, keep_x=False)[0, :S].reshape(S, 1, D)
    return (y_prompt, y_sample, jnp.stack(new_kp), jnp.stack(new_vp), jnp.stack(new_cp),
            jnp.stack(new_ks), jnp.stack(new_vs), jnp.stack(new_cs))
```
